```python
import jax, jax.numpy as jnp
from jax import lax
import numpy as np

D_MODEL = 2048
BATCH = 4
SEQ = 4096
DEPTH = 2

HEAD_DIM = 64
D_SB = D_MODEL // 4
D_FOX = D_MODEL // 2
CONV_CH = D_MODEL // 4
N_SB_HEADS = D_SB // HEAD_DIM
N_FOX_HEADS = D_FOX // HEAD_DIM
CONV_WIDTH = 31
MIX_COLS = 3 * D_SB + 3 * D_FOX + N_FOX_HEADS + 2 * CONV_CH
Q_BLOCK = 128

N_GROUPS = 4
EXPERTS_PER_GROUP = 8
N_EXPERTS = N_GROUPS * EXPERTS_PER_GROUP
TOP_K = 2
D_EXPERT = D_MODEL // 2
MOE_BLOCK = 256

ALPHA = (2 * DEPTH) ** 0.25
INIT_BETA = (8 * DEPTH) ** -0.25
LN_EPS = 1e-5

kernel_name = 'hybrid_sb_fox_conformer_hmoe_deepnorm_adaln'


def _ln(x):
    xf = x.astype(jnp.float32)
    mu = jnp.mean(xf, axis=-1, keepdims=True)
    var = jnp.mean(jnp.square(xf - mu), axis=-1, keepdims=True)
    return ((xf - mu) * lax.rsqrt(var + LN_EPS)).astype(x.dtype)


def _heads(t, n):
    b, s, _ = t.shape
    return t.reshape(b, s, n, HEAD_DIM).transpose(0, 2, 1, 3)


def _merge_blocks(o):
    nb, b, h, blk, dh = o.shape
    return o.transpose(1, 0, 3, 2, 4).reshape(b, nb * blk, h * dh)


def stick_breaking_attention(q, k, v):
    s = q.shape[2]
    scale = HEAD_DIM ** -0.5
    key_pos = jnp.arange(s)

    def block(i):
        q_blk = lax.dynamic_slice_in_dim(q, i * Q_BLOCK, Q_BLOCK, axis=2)
        z = jnp.einsum('bhqd,bhkd->bhqk', q_blk, k).astype(jnp.float32) * scale
        q_pos = i * Q_BLOCK + jnp.arange(Q_BLOCK)
        mask = key_pos[None, :] < q_pos[:, None]
        log_keep = jnp.where(mask, jax.nn.log_sigmoid(-z), 0.0)
        later = lax.cumsum(log_keep, axis=3, reverse=True) - log_keep
        w = jnp.where(mask, jnp.exp(jax.nn.log_sigmoid(z) + later), 0.0)
        return jnp.einsum('bhqk,bhkd->bhqd', w.astype(v.dtype), v)

    return _merge_blocks(lax.map(block, jnp.arange(s // Q_BLOCK)))


def forgetting_attention(q, k, v, log_f):
    s = q.shape[2]
    scale = HEAD_DIM ** -0.5
    key_pos = jnp.arange(s)
    cum = jnp.cumsum(log_f, axis=-1)

    def block(i):
        q_blk = lax.dynamic_slice_in_dim(q, i * Q_BLOCK, Q_BLOCK, axis=2)
        cum_q = lax.dynamic_slice_in_dim(cum, i * Q_BLOCK, Q_BLOCK, axis=2)
        z = jnp.einsum('bhqd,bhkd->bhqk', q_blk, k).astype(jnp.float32) * scale
        logits = z + cum_q[..., :, None] - cum[..., None, :]
        q_pos = i * Q_BLOCK + jnp.arange(Q_BLOCK)
        mask = key_pos[None, :] <= q_pos[:, None]
        p = jax.nn.softmax(jnp.where(mask, logits, -jnp.inf), axis=-1)
        return jnp.einsum('bhqk,bhkd->bhqd', p.astype(v.dtype), v)

    return _merge_blocks(lax.map(block, jnp.arange(s // Q_BLOCK)))


def conformer_conv(a, g, conv_w, conv_b, ln_g, ln_b):
    u = a * jax.nn.sigmoid(g)
    u = lax.conv_general_dilated(
        u, conv_w[:, None, :].astype(u.dtype), window_strides=(1,),
        padding=[(CONV_WIDTH - 1, 0)], dimension_numbers=('NWC', 'WIO', 'NWC'),
        feature_group_count=u.shape[-1]) + conv_b
    u = _ln(u) * ln_g + ln_b
    return jax.nn.silu(u)


def hybrid_mixer(h, w_in, b_forget, conv_w, conv_b, conv_ln_g, conv_ln_b, w_out):
    sizes = [D_SB, D_SB, D_SB, D_FOX, D_FOX, D_FOX, N_FOX_HEADS, CONV_CH, CONV_CH]
    offsets = [int(o) for o in np.cumsum(sizes)[:-1]]
    proj = h @ w_in
    q_sb, k_sb, v_sb, q_fx, k_fx, v_fx, f_logit, glu_a, glu_g = jnp.split(proj, offsets, axis=-1)

    o_sb = stick_breaking_attention(_heads(q_sb, N_SB_HEADS), _heads(k_sb, N_SB_HEADS),
                                    _heads(v_sb, N_SB_HEADS))
    log_f = jax.nn.log_sigmoid((f_logit + b_forget).astype(jnp.float32)).transpose(0, 2, 1)
    o_fx = forgetting_attention(_heads(q_fx, N_FOX_HEADS), _heads(k_fx, N_FOX_HEADS),
                                _heads(v_fx, N_FOX_HEADS), log_f)
    o_cv = conformer_conv(glu_a, glu_g, conv_w, conv_b, conv_ln_g, conv_ln_b)
    return jnp.concatenate([o_sb, o_fx, o_cv], axis=-1) @ w_out


def hierarchical_moe(h, r1_w, r1_b, r2_w, r2_b, w_gate, w_up, w_down):
    b, s, d = h.shape
    ht = h.reshape(b * s, d)
    n_tok = ht.shape[0]
    logits1 = (ht @ r1_w + r1_b).astype(jnp.float32)
    p1 = jax.nn.softmax(logits1, axis=-1)
    grp = jnp.argmax(logits1, axis=-1)
    p_grp = jnp.take_along_axis(p1, grp[:, None], axis=1)
    logits2 = (jnp.einsum('td,gde->tge', ht, r2_w) + r2_b).astype(jnp.float32)
    logits2 = jnp.take_along_axis(logits2, grp[:, None, None], axis=1)[:, 0]
    top_p, top_e = lax.top_k(jax.nn.softmax(logits2, axis=-1), TOP_K)
    weights = (p_grp * top_p / jnp.sum(top_p, axis=-1, keepdims=True)).astype(h.dtype)
    expert = grp[:, None] * EXPERTS_PER_GROUP + top_e

    flat_e = expert.reshape(-1)
    n_assign = flat_e.shape[0]
    order = jnp.argsort(flat_e)
    sorted_e = flat_e[order]
    counts = jnp.bincount(flat_e, length=N_EXPERTS)
    padded = (counts + MOE_BLOCK - 1) // MOE_BLOCK * MOE_BLOCK
    pad_start = jnp.cumsum(padded) - padded
    seg_start = jnp.cumsum(counts) - counts
    dest = pad_start[sorted_e] + jnp.arange(n_assign) - seg_start[sorted_e]
    n_blocks = -(-n_assign // MOE_BLOCK) + N_EXPERTS
    buf_tok = jnp.zeros((n_blocks * MOE_BLOCK,), jnp.int32).at[dest].set((order // TOP_K).astype(jnp.int32))
    block_expert = jnp.zeros((n_blocks,), jnp.int32).at[dest // MOE_BLOCK].set(sorted_e.astype(jnp.int32))
    xb = ht[buf_tok].reshape(n_blocks, MOE_BLOCK, d)

    def expert_block(args):
        xe, e = args
        hid = jax.nn.silu(xe @ w_gate[e]) * (xe @ w_up[e])
        return hid @ w_down[e]

    yb = lax.map(expert_block, (xb, block_expert)).reshape(n_blocks * MOE_BLOCK, d)
    y_assign = jnp.zeros((n_assign, d), h.dtype).at[order].set(yb[dest])
    y = jnp.sum(y_assign.reshape(n_tok, TOP_K, d) * weights[..., None], axis=1)
    return y.reshape(b, s, d)


def setup_inputs(seed: int = 0) -> dict:
    key = jax.random.key(seed)
    ks = jax.random.split(key, 24)
    L, D = DEPTH, D_MODEL
    nrm = lambda k, shape, sc: jax.random.normal(k, shape, jnp.float32) * sc
    return {
        'x': nrm(ks[0], (BATCH, SEQ, D), 1.0),
        'c': nrm(ks[1], (BATCH, D), 1.0),
        'ada_w': nrm(ks[2], (L, D, 6 * D), D ** -0.5),
        'ada_b': nrm(ks[3], (L, 6 * D), 0.01),
        'w_in': nrm(ks[4], (L, D, MIX_COLS), D ** -0.5),
        'b_forget': 1.0 + 4.0 * jax.random.uniform(ks[5], (L, N_FOX_HEADS), jnp.float32),
        'conv_w': nrm(ks[6], (L, CONV_WIDTH, CONV_CH), CONV_WIDTH ** -0.5),
        'conv_b': nrm(ks[7], (L, CONV_CH), 0.01),
        'conv_ln_g': 1.0 + nrm(ks[8], (L, CONV_CH), 0.05),
        'conv_ln_b': nrm(ks[9], (L, CONV_CH), 0.01),
        'w_out': nrm(ks[10], (L, D, D), D ** -0.5 * INIT_BETA),
        'ln1_g': 1.0 + nrm(ks[11], (L, D), 0.05),
        'ln1_b': nrm(ks[12], (L, D), 0.01),
        'r1_w': nrm(ks[13], (L, D, N_GROUPS), D ** -0.5),
        'r1_b': nrm(ks[14], (L, N_GROUPS), 0.01),
        'r2_w': nrm(ks[15], (L, N_GROUPS, D, EXPERTS_PER_GROUP), D ** -0.5),
        'r2_b': nrm(ks[16], (L, N_GROUPS, EXPERTS_PER_GROUP), 0.01),
        'w_gate': nrm(ks[17], (L, N_EXPERTS, D, D_EXPERT), D ** -0.5),
        'w_up': nrm(ks[18], (L, N_EXPERTS, D, D_EXPERT), D ** -0.5),
        'w_down': nrm(ks[19], (L, N_EXPERTS, D_EXPERT, D), D_EXPERT ** -0.5 * INIT_BETA),
        'ln2_g': 1.0 + nrm(ks[20], (L, D), 0.05),
        'ln2_b': nrm(ks[21], (L, D), 0.01),
    }


def reference(x, c, ada_w, ada_b, w_in, b_forget, conv_w, conv_b, conv_ln_g, conv_ln_b,
              w_out, ln1_g, ln1_b, r1_w, r1_b, r2_w, r2_b, w_gate, w_up, w_down,
              ln2_g, ln2_b):
    for l in range(DEPTH):
        mod = jax.nn.silu(c) @ ada_w[l] + ada_b[l]
        sh1, sc1, gt1, sh2, sc2, gt2 = [m[:, None, :] for m in jnp.split(mod, 6, axis=-1)]
        h = _ln(x) * (1.0 + sc1) + sh1
        y = hybrid_mixer(h, w_in[l], b_forget[l], conv_w[l], conv_b[l], conv_ln_g[l],
                         conv_ln_b[l], w_out[l])
        x = _ln(ALPHA * x + gt1 * y) * ln1_g[l] + ln1_b[l]
        h = _ln(x) * (1.0 + sc2) + sh2
        y = hierarchical_moe(h, r1_w[l], r1_b[l], r2_w[l], r2_b[l], w_gate[l], w_up[l], w_down[l])
        x = _ln(ALPHA * x + gt2 * y) * ln2_g[l] + ln2_b[l]
    return x
```

```python
import functools

import jax
import jax.numpy as jnp
from jax import lax
from jax.experimental import pallas as pl
from jax.experimental.pallas import tpu as pltpu

LN_EPS = 1e-5
HEAD_DIM = 64
LANES = 128
SUBLANES = 8
MOE_BLOCK = 256
NEG_BIG = -1e30
VMEM_LIMIT = 56 * 1024 * 1024

F32 = jnp.float32
BF16 = jnp.bfloat16


def _pick(n, cands):
    for c in cands:
        if n % c == 0:
            return c
    return n


def _params(sem):
    return pltpu.CompilerParams(dimension_semantics=sem, vmem_limit_bytes=VMEM_LIMIT)


def _ln_rows(v):
    mu = jnp.mean(v, axis=-1, keepdims=True)
    d = v - mu
    var = jnp.mean(d * d, axis=-1, keepdims=True)
    return d * lax.rsqrt(var + LN_EPS)


def _log_sigmoid(z):
    return jnp.minimum(z, 0.0) - jnp.log1p(jnp.exp(-jnp.abs(z)))


def _sigmoid(z):
    return 1.0 / (1.0 + jnp.exp(-z))


def _split2(v):
    hi = v.astype(BF16)
    lo = (v - hi.astype(F32)).astype(BF16)
    return hi, lo


def _dot(a, b):
    return jnp.dot(a, b, preferred_element_type=F32)


def _dot_nt(a, b):
    return lax.dot_general(a, b, (((1,), (1,)), ((), ())), preferred_element_type=F32)


def _ada_kernel(c_ref, w_ref, b_ref, o_ref):
    c = c_ref[...]
    s = c * _sigmoid(c)
    s_hi, s_lo = _split2(s)
    w_hi, w_lo = _split2(w_ref[0])
    acc = _dot(s_hi, w_hi) + _dot(s_lo, w_hi) + _dot(s_hi, w_lo)
    o_ref[0] = acc + b_ref[0]


def _ada(c_pad, ada_w, ada_b):
    L, D, N = ada_w.shape
    rows = c_pad.shape[0]
    tn = _pick(N, (512, 256, 128))
    return pl.pallas_call(
        _ada_kernel,
        grid=(L, N // tn),
        in_specs=[
            pl.BlockSpec((rows, D), lambda l, n: (0, 0)),
            pl.BlockSpec((1, D, tn), lambda l, n: (l, 0, n)),
            pl.BlockSpec((1, 1, tn), lambda l, n: (l, 0, n)),
        ],
        out_specs=pl.BlockSpec((1, rows, tn), lambda l, n: (l, 0, n)),
        out_shape=jax.ShapeDtypeStruct((L, rows, N), F32),
        compiler_params=_params(("arbitrary", "arbitrary")),
        name="ada_mod",
    )(c_pad, ada_w, ada_b.reshape(L, 1, N))


def _lnmod_kernel(x_ref, sc_ref, sh_ref, o_ref):
    h = _ln_rows(x_ref[...]) * (1.0 + sc_ref[0]) + sh_ref[0]
    o_ref[...] = h.astype(o_ref.dtype)


def _lnmod(x2d, sc, sh, seq):
    T, D = x2d.shape
    tm = _pick(seq, (512, 256, 128))
    per_b = seq // tm
    return pl.pallas_call(
        _lnmod_kernel,
        grid=(T // tm,),
        in_specs=[
            pl.BlockSpec((tm, D), lambda i: (i, 0)),
            pl.BlockSpec((1, 1, D), lambda i: (i // per_b, 0, 0)),
            pl.BlockSpec((1, 1, D), lambda i: (i // per_b, 0, 0)),
        ],
        out_specs=pl.BlockSpec((tm, D), lambda i: (i, 0)),
        out_shape=jax.ShapeDtypeStruct((T, D), BF16),
        compiler_params=_params(("arbitrary",)),
        name="ln_mod",
    )(x2d, sc, sh)


def _mm_kernel(a_ref, b_ref, o_ref):
    o_ref[...] = _dot(a_ref[...], b_ref[...]).astype(o_ref.dtype)


def _matmul(a, w, out_dtype, name):
    M, K = a.shape
    N = w.shape[1]
    tm = _pick(M, (1024, 512, 256, 128))
    tn = _pick(N, (512, 384, 256, 128))
    return pl.pallas_call(
        _mm_kernel,
        grid=(M // tm, N // tn),
        in_specs=[
            pl.BlockSpec((tm, K), lambda i, j: (i, 0)),
            pl.BlockSpec((K, tn), lambda i, j: (0, j)),
        ],
        out_specs=pl.BlockSpec((tm, tn), lambda i, j: (i, j)),
        out_shape=jax.ShapeDtypeStruct((M, N), out_dtype),
        compiler_params=_params(("arbitrary", "arbitrary")),
        name=name,
    )(a, w)


def _cum_kernel(f_ref, b_ref, o_ref, carry_ref):
    @pl.when(pl.program_id(1) == 0)
    def _():
        carry_ref[...] = jnp.zeros_like(carry_ref)

    ts = f_ref.shape[0]
    lf = _log_sigmoid(f_ref[...] + b_ref[...])
    p1 = lf.astype(BF16)
    r1 = lf - p1.astype(F32)
    p2 = r1.astype(BF16)
    p3 = (r1 - p2.astype(F32)).astype(BF16)
    row = lax.broadcasted_iota(jnp.int32, (ts, ts), 0)
    col = lax.broadcasted_iota(jnp.int32, (ts, ts), 1)
    tri = jnp.where(col <= row, 1.0, 0.0).astype(BF16)
    cum = _dot(tri, p1) + _dot(tri, p2) + _dot(tri, p3) + carry_ref[...]
    o_ref[...] = cum
    carry_ref[...] = cum[ts - 1:ts, :]


def _forget_cumsum(rest, b_pad, batch, seq, col_block):
    T = rest.shape[0]
    ts = _pick(seq, (256, 128))
    ns = seq // ts
    return pl.pallas_call(
        _cum_kernel,
        grid=(batch, ns),
        in_specs=[
            pl.BlockSpec((ts, LANES), lambda b, s: (b * ns + s, col_block)),
            pl.BlockSpec((1, LANES), lambda b, s: (0, 0)),
        ],
        out_specs=pl.BlockSpec((ts, LANES), lambda b, s: (b * ns + s, 0)),
        out_shape=jax.ShapeDtypeStruct((T, LANES), F32),
        scratch_shapes=[pltpu.VMEM((1, LANES), F32)],
        compiler_params=_params(("arbitrary", "arbitrary")),
        name="forget_cumsum",
    )(rest, b_pad)


def _sb_kernel(q_ref, k_ref, v_ref, o_ref):
    tq = q_ref.shape[0]
    i = pl.program_id(2)
    lane = lax.broadcasted_iota(jnp.int32, (1, LANES), 1)
    q2 = q_ref[...] * jnp.asarray(HEAD_DIM ** -0.5, q_ref.dtype)
    zero = jnp.zeros_like(q2)
    q_heads = (jnp.where(lane < HEAD_DIM, q2, zero), jnp.where(lane >= HEAD_DIM, q2, zero))
    row = lax.broadcasted_iota(jnp.int32, (tq, tq), 0)
    col = lax.broadcasted_iota(jnp.int32, (tq, tq), 1)
    suffix = jnp.where(row > col, 1.0, 0.0).astype(BF16)

    def body(kk, carry):
        j = i - kk
        start = pl.multiple_of(j * tq, tq)
        kblk = k_ref[pl.ds(start, tq), :]
        vblk = v_ref[pl.ds(start, tq), :]
        valid = (col + j * tq) < (row + i * tq)
        out = []
        for h in range(2):
            run, acc = carry[2 * h], carry[2 * h + 1]
            z = _dot_nt(q_heads[h], kblk)
            ls = _log_sigmoid(z)
            lk = jnp.where(valid, ls - z, 0.0)
            lk_hi, lk_lo = _split2(lk)
            later = _dot(lk_hi, suffix) + _dot(lk_lo, suffix)
            w = jnp.where(valid, jnp.exp(ls + later + run), 0.0)
            acc = acc + _dot(w.astype(BF16), vblk)
            run = run + jnp.sum(lk, axis=1, keepdims=True)
            out += [run, acc]
        return tuple(out)

    init = (jnp.zeros((tq, 1), F32), jnp.zeros((tq, LANES), F32)) * 2
    res = lax.fori_loop(0, i + 1, body, init)
    o_ref[...] = jnp.where(lane < HEAD_DIM, res[1], res[3]).astype(o_ref.dtype)


def _sb_attention(qkv, batch, seq, n_pairs, q_col, k_col, v_col):
    T = qkv.shape[0]
    tq = _pick(seq, (256, 128))
    nq = seq // tq
    return pl.pallas_call(
        _sb_kernel,
        grid=(batch, n_pairs, nq),
        in_specs=[
            pl.BlockSpec((tq, LANES), lambda b, p, i: (b * nq + i, q_col + p)),
            pl.BlockSpec((seq, LANES), lambda b, p, i: (b, k_col + p)),
            pl.BlockSpec((seq, LANES), lambda b, p, i: (b, v_col + p)),
        ],
        out_specs=pl.BlockSpec((tq, LANES), lambda b, p, i: (b * nq + i, p)),
        out_shape=jax.ShapeDtypeStruct((T, n_pairs * LANES), BF16),
        compiler_params=_params(("arbitrary", "arbitrary", "arbitrary")),
        name="sb_attention",
    )(qkv, qkv, qkv)


def _fox_kernel(q_ref, k_ref, v_ref, cq_ref, ck_ref, o_ref):
    tq = q_ref.shape[0]
    i = pl.program_id(2)
    lane = lax.broadcasted_iota(jnp.int32, (1, LANES), 1)
    q2 = q_ref[...] * jnp.asarray(HEAD_DIM ** -0.5, q_ref.dtype)
    zero = jnp.zeros_like(q2)
    q_heads = (jnp.where(lane < HEAD_DIM, q2, zero), jnp.where(lane >= HEAD_DIM, q2, zero))
    cq = cq_ref[0]
    cq_heads = (cq[:, 0:1], cq[:, 1:2])
    row = lax.broadcasted_iota(jnp.int32, (tq, tq), 0)
    col = lax.broadcasted_iota(jnp.int32, (tq, tq), 1)

    def body(j, carry):
        start = pl.multiple_of(j * tq, tq)
        kblk = k_ref[pl.ds(start, tq), :]
        vblk = v_ref[pl.ds(start, tq), :]
        ck = ck_ref[0, 0, j]
        valid = (col + j * tq) <= (row + i * tq)
        out = []
        for h in range(2):
            m, l, acc = carry[3 * h], carry[3 * h + 1], carry[3 * h + 2]
            logits = _dot_nt(q_heads[h], kblk) + cq_heads[h] - ck[h:h + 1, :]
            logits = jnp.where(valid, logits, NEG_BIG)
            m_new = jnp.maximum(m, jnp.max(logits, axis=1, keepdims=True))
            alpha = jnp.exp(m - m_new)
            p = jnp.exp(logits - m_new)
            l = l * alpha + jnp.sum(p, axis=1, keepdims=True)
            acc = acc * alpha + _dot(p.astype(BF16), vblk)
            out += [m_new, l, acc]
        return tuple(out)

    init = (jnp.full((tq, 1), NEG_BIG, F32), jnp.zeros((tq, 1), F32),
            jnp.zeros((tq, LANES), F32)) * 2
    res = lax.fori_loop(0, i + 1, body, init)
    o = jnp.where(lane < HEAD_DIM, res[2] / res[1], res[5] / res[4])
    o_ref[...] = o.astype(o_ref.dtype)


def _fox_attention(qkv, cq, ck, batch, seq, n_pairs, q_col, k_col, v_col):
    T = qkv.shape[0]
    tq = ck.shape[-1]
    nq = seq // tq
    return pl.pallas_call(
        _fox_kernel,
        grid=(batch, n_pairs, nq),
        in_specs=[
            pl.BlockSpec((tq, LANES), lambda b, p, i: (b * nq + i, q_col + p)),
            pl.BlockSpec((seq, LANES), lambda b, p, i: (b, k_col + p)),
            pl.BlockSpec((seq, LANES), lambda b, p, i: (b, v_col + p)),
            pl.BlockSpec((1, tq, 2), lambda b, p, i: (p, b * nq + i, 0)),
            pl.BlockSpec((1, 1, nq, 2, tq), lambda b, p, i: (b, p, 0, 0, 0)),
        ],
        out_specs=pl.BlockSpec((tq, LANES), lambda b, p, i: (b * nq + i, p)),
        out_shape=jax.ShapeDtypeStruct((T, n_pairs * LANES), BF16),
        compiler_params=_params(("arbitrary", "arbitrary", "arbitrary")),
        name="fox_attention",
    )(qkv, qkv, qkv, cq, ck)


def _conv_kernel(a_ref, g_ref, w_ref, cb_ref, lg_ref, lb_ref, o_ref, u_ref, *, width, halo):
    ts = a_ref.shape[0]

    @pl.when(pl.program_id(1) == 0)
    def _():
        u_ref[0:halo, :] = jnp.zeros((halo, u_ref.shape[1]), F32)

    u_ref[halo:halo + ts, :] = a_ref[...] * _sigmoid(g_ref[...])
    acc = jnp.zeros(a_ref.shape, F32) + cb_ref[...]
    for k in range(width):
        off = halo - (width - 1) + k
        acc = acc + w_ref[k:k + 1, :] * u_ref[off:off + ts, :]
    y = _ln_rows(acc) * lg_ref[...] + lb_ref[...]
    o_ref[...] = (y * _sigmoid(y)).astype(o_ref.dtype)
    u_ref[0:halo, :] = u_ref[ts:ts + halo, :]


def _conv_module(rest, conv_w, conv_b, ln_g, ln_b, batch, seq):
    T = rest.shape[0]
    width, C = conv_w.shape
    halo = -(-(width - 1) // SUBLANES) * SUBLANES
    ts = _pick(seq, (128,))
    ns = seq // ts
    kern = functools.partial(_conv_kernel, width=width, halo=halo)
    vec = lambda: pl.BlockSpec((1, C), lambda b, s: (0, 0))
    return pl.pallas_call(
        kern,
        grid=(batch, ns),
        in_specs=[
            pl.BlockSpec((ts, C), lambda b, s: (b * ns + s, 0)),
            pl.BlockSpec((ts, C), lambda b, s: (b * ns + s, 1)),
            pl.BlockSpec((width, C), lambda b, s: (0, 0)),
            vec(), vec(), vec(),
        ],
        out_specs=pl.BlockSpec((ts, C), lambda b, s: (b * ns + s, 0)),
        out_shape=jax.ShapeDtypeStruct((T, C), BF16),
        scratch_shapes=[pltpu.VMEM((ts + halo, C), F32)],
        compiler_params=_params(("arbitrary", "arbitrary")),
        name="conformer_conv",
    )(rest, rest, conv_w, conv_b.reshape(1, C), ln_g.reshape(1, C), ln_b.reshape(1, C))


def _outproj_kernel(osb_ref, ofx_ref, ocv_ref, w_ref, x_ref, gt_ref, g_ref, b_ref,
                    sc_ref, sh_ref, rwh_ref, rwl_ref, rb_ref,
                    x1_ref, h2_ref, lg_ref, *, alpha):
    d_sb = osb_ref.shape[1]
    d_fx = ofx_ref.shape[1]
    y = _dot(osb_ref[...], w_ref[0:d_sb, :])
    y = y + _dot(ofx_ref[...], w_ref[d_sb:d_sb + d_fx, :])
    y = y + _dot(ocv_ref[...], w_ref[d_sb + d_fx:, :])
    x1 = _ln_rows(alpha * x_ref[...] + gt_ref[0] * y) * g_ref[...] + b_ref[...]
    x1_ref[...] = x1
    h2 = _ln_rows(x1) * (1.0 + sc_ref[0]) + sh_ref[0]
    h2_ref[...] = h2
    h_hi, h_lo = _split2(h2)
    lg = _dot(h_hi, rwh_ref[...]) + _dot(h_lo, rwh_ref[...]) + _dot(h_hi, rwl_ref[...])
    lg_ref[...] = lg + rb_ref[...]


def _outproj(o_sb, o_fx, o_cv, w_out, x2d, gt, g, b, sc2, sh2, rw_hi, rw_lo, rb, seq, alpha):
    T, D = x2d.shape
    tm = _pick(seq, (256, 128))
    per_b = seq // tm
    rowblk = lambda n: pl.BlockSpec((tm, n), lambda i: (i, 0))
    full = lambda r, c: pl.BlockSpec((r, c), lambda i: (0, 0))
    perb = lambda: pl.BlockSpec((1, 1, D), lambda i: (i // per_b, 0, 0))
    return pl.pallas_call(
        functools.partial(_outproj_kernel, alpha=alpha),
        grid=(T // tm,),
        in_specs=[
            rowblk(o_sb.shape[1]), rowblk(o_fx.shape[1]), rowblk(o_cv.shape[1]),
            full(D, D), rowblk(D), perb(), full(1, D), full(1, D), perb(), perb(),
            full(D, LANES), full(D, LANES), full(1, LANES),
        ],
        out_specs=[rowblk(D), rowblk(D), rowblk(LANES)],
        out_shape=[jax.ShapeDtypeStruct((T, D), F32), jax.ShapeDtypeStruct((T, D), F32),
                   jax.ShapeDtypeStruct((T, LANES), F32)],
        compiler_params=_params(("arbitrary",)),
        name="out_proj_norm_router",
    )(o_sb, o_fx, o_cv, w_out, x2d, gt, g, b, sc2, sh2, rw_hi, rw_lo, rb)


def _route_kernel(lg_ref, e_ref, w_ref, *, n_groups, epg):
    L = lg_ref[...]
    lane_i = lax.broadcasted_iota(jnp.int32, L.shape, 1)
    lane = lane_i.astype(F32)
    far = 1e6
    is_grp = lane < n_groups
    l1 = jnp.where(is_grp, L, NEG_BIG)
    m1 = jnp.max(l1, axis=1, keepdims=True)
    grp = jnp.min(jnp.where(is_grp & (l1 == m1), lane, far), axis=1, keepdims=True)
    s1 = jnp.sum(jnp.where(is_grp, jnp.exp(l1 - m1), 0.0), axis=1, keepdims=True)
    p_grp = 1.0 / s1
    lo = n_groups + grp * epg
    in_grp = (lane >= lo) & (lane < lo + epg)
    l2 = jnp.where(in_grp, L, NEG_BIG)
    m2 = jnp.max(l2, axis=1, keepdims=True)
    e2 = jnp.where(in_grp, jnp.exp(l2 - m2), -1.0)
    v1 = jnp.max(e2, axis=1, keepdims=True)
    i1 = jnp.min(jnp.where(e2 == v1, lane, far), axis=1, keepdims=True)
    e2b = jnp.where(lane == i1, -1.0, e2)
    v2 = jnp.max(e2b, axis=1, keepdims=True)
    i2 = jnp.min(jnp.where(e2b == v2, lane, far), axis=1, keepdims=True)
    den = v1 + v2
    w0 = p_grp * v1 / den
    w1 = p_grp * v2 / den
    ids = jnp.where(lane_i == 0, i1 - n_groups, jnp.where(lane_i == 1, i2 - n_groups, 0.0))
    e_ref[...] = ids.astype(jnp.int32)
    w_ref[...] = jnp.where(lane_i == 0, w0, jnp.where(lane_i == 1, w1, 0.0))


def _route(logits, n_groups, epg):
    T = logits.shape[0]
    tm = _pick(T, (512, 256, 128))
    blk = lambda: pl.BlockSpec((tm, LANES), lambda i: (i, 0))
    return pl.pallas_call(
        functools.partial(_route_kernel, n_groups=n_groups, epg=epg),
        grid=(T // tm,),
        in_specs=[blk()],
        out_specs=[blk(), blk()],
        out_shape=[jax.ShapeDtypeStruct((T, LANES), jnp.int32),
                   jax.ShapeDtypeStruct((T, LANES), F32)],
        compiler_params=_params(("arbitrary",)),
        name="route_topk",
    )(logits)


def _plan_kernel(e_ref, dest_ref, be_ref, cnt_ref, base_ref, *, n_experts):
    p = pl.program_id(0)
    i = pl.program_id(1)
    tm = e_ref.shape[0]
    lane = lax.broadcasted_iota(jnp.int32, (tm, LANES), 1)
    e = e_ref[...]
    hot0 = lane == e[:, 0:1]
    hot1 = lane == e[:, 1:2]
    both = jnp.where(hot0 | hot1, 1.0, 0.0)

    @pl.when((p == 0) & (i == 0))
    def _():
        cnt_ref[...] = jnp.zeros_like(cnt_ref)

    @pl.when(p == 0)
    def _():
        cnt_ref[...] += jnp.sum(both, axis=0, keepdims=True)

    @pl.when((p == 1) & (i == 0))
    def _():
        nblk = jnp.floor((cnt_ref[...] + (MOE_BLOCK - 1)) * (1.0 / MOE_BLOCK))
        r = lax.broadcasted_iota(jnp.int32, (LANES, LANES), 0)
        c = lax.broadcasted_iota(jnp.int32, (LANES, LANES), 1)
        before = jnp.where(r < c, 1.0, 0.0).astype(BF16)
        nb8 = jnp.broadcast_to(nblk, (SUBLANES, LANES)).astype(BF16)
        excl = _dot(nb8, before)[0:1, :]
        base_ref[...] = excl * MOE_BLOCK
        cnt_ref[...] = jnp.zeros_like(cnt_ref)
        incl = excl + nblk
        nb_rows = be_ref.shape[0]
        bidx = lax.broadcasted_iota(jnp.int32, (nb_rows, LANES), 0).astype(F32)
        lane_b = lax.broadcasted_iota(jnp.int32, (nb_rows, LANES), 1)
        done = jnp.where((incl <= bidx) & (lane_b < n_experts), 1.0, 0.0)
        be = jnp.sum(done, axis=1, keepdims=True)
        be_ref[...] = jnp.broadcast_to(be, (nb_rows, LANES)).astype(jnp.int32)

    @pl.when(p == 1)
    def _():
        r = lax.broadcasted_iota(jnp.int32, (tm, tm), 0)
        c = lax.broadcasted_iota(jnp.int32, (tm, tm), 1)
        earlier = jnp.where(c < r, 1.0, 0.0).astype(BF16)
        tot = _dot(earlier, both.astype(BF16)) + cnt_ref[...] + base_ref[...]
        d0 = jnp.sum(jnp.where(hot0, tot, 0.0), axis=1, keepdims=True)
        d1 = jnp.sum(jnp.where(hot1, tot, 0.0), axis=1, keepdims=True)
        dest = jnp.where(lane == 0, d0, jnp.where(lane == 1, d1, 0.0))
        dest_ref[...] = dest.astype(jnp.int32)
        cnt_ref[...] += jnp.sum(both, axis=0, keepdims=True)


def _plan(e_lanes, n_experts, n_blocks):
    T = e_lanes.shape[0]
    tm = _pick(T, (256, 128))
    nb_rows = -(-n_blocks // SUBLANES) * SUBLANES
    return pl.pallas_call(
        functools.partial(_plan_kernel, n_experts=n_experts),
        grid=(2, T // tm),
        in_specs=[pl.BlockSpec((tm, LANES), lambda p, i: (i, 0))],
        out_specs=[pl.BlockSpec((tm, LANES), lambda p, i: (i * p, 0)),
                   pl.BlockSpec((nb_rows, LANES), lambda p, i: (0, 0))],
        out_shape=[jax.ShapeDtypeStruct((T, LANES), jnp.int32),
                   jax.ShapeDtypeStruct((nb_rows, LANES), jnp.int32)],
        scratch_shapes=[pltpu.VMEM((1, LANES), F32), pltpu.VMEM((1, LANES), F32)],
        compiler_params=_params(("arbitrary", "arbitrary")),
        name="dispatch_plan",
    )(e_lanes)


def _row_copy(src, src_row, dst, dst_row, sem):
    return pltpu.make_async_copy(src.at[pl.ds(src_row, 1)], dst.at[pl.ds(dst_row, 1)], sem)


def _dispatch_kernel(dest_ref, h_ref, xb_in_ref, xb_ref, sem, *, tm):
    del xb_in_ref
    base = pl.program_id(0) * tm

    def issue(r, carry):
        t = base + r
        _row_copy(h_ref, t, xb_ref, dest_ref[2 * t], sem).start()
        _row_copy(h_ref, t, xb_ref, dest_ref[2 * t + 1], sem).start()
        return carry

    lax.fori_loop(0, tm, issue, 0)

    def drain(r, carry):
        _row_copy(h_ref, 0, xb_ref, 0, sem).wait()
        return carry

    lax.fori_loop(0, 2 * tm, drain, 0)


def _dispatch(dest_flat, h2, xb_zero):
    T, D = h2.shape
    tm = _pick(T, (512, 256, 128))
    return pl.pallas_call(
        functools.partial(_dispatch_kernel, tm=tm),
        grid_spec=pltpu.PrefetchScalarGridSpec(
            num_scalar_prefetch=1,
            grid=(T // tm,),
            in_specs=[pl.BlockSpec(memory_space=pl.ANY), pl.BlockSpec(memory_space=pl.ANY)],
            out_specs=pl.BlockSpec(memory_space=pl.ANY),
            scratch_shapes=[pltpu.SemaphoreType.DMA],
        ),
        out_shape=jax.ShapeDtypeStruct(xb_zero.shape, xb_zero.dtype),
        input_output_aliases={2: 0},
        compiler_params=_params(("arbitrary",)),
        name="moe_dispatch",
    )(dest_flat, h2, xb_zero)


def _expert_kernel(be_ref, nu_ref, x_ref, wg_ref, wu_ref, wd_ref, o_ref):
    del be_ref
    used = pl.program_id(0) < nu_ref[0]

    @pl.when(used)
    def _():
        x = x_ref[...].astype(BF16)
        g = _dot(x, wg_ref[0])
        u = _dot(x, wu_ref[0])
        hid = (g * _sigmoid(g)) * u
        o_ref[...] = _dot(hid.astype(BF16), wd_ref[0])

    @pl.when(jnp.logical_not(used))
    def _():
        o_ref[...] = jnp.zeros_like(o_ref)


def _experts(block_expert, n_used, xb, w_gate, w_up, w_down, n_blocks):
    D = xb.shape[1]
    DE = w_gate.shape[2]
    blk = lambda b, be, nu: (jnp.minimum(b, nu[0] - 1), 0)
    wmap = lambda b, be, nu: (be[jnp.minimum(b, nu[0] - 1)], 0, 0)
    return pl.pallas_call(
        _expert_kernel,
        grid_spec=pltpu.PrefetchScalarGridSpec(
            num_scalar_prefetch=2,
            grid=(n_blocks,),
            in_specs=[
                pl.BlockSpec((MOE_BLOCK, D), blk),
                pl.BlockSpec((1, D, DE), wmap),
                pl.BlockSpec((1, D, DE), wmap),
                pl.BlockSpec((1, DE, D), wmap),
            ],
            out_specs=pl.BlockSpec((MOE_BLOCK, D), lambda b, be, nu: (b, 0)),
        ),
        out_shape=jax.ShapeDtypeStruct(xb.shape, F32),
        compiler_params=_params(("arbitrary",)),
        name="moe_experts",
    )(block_expert, n_used, xb, w_gate, w_up, w_down)


def _combine_kernel(dest_ref, yb_ref, w_ref, x_ref, gt_ref, g_ref, b_ref, o_ref,
                    buf_ref, sem, *, alpha):
    tm = x_ref.shape[0]
    base = pl.program_id(0) * tm

    def issue(r, carry):
        t = base + r
        _row_copy(yb_ref, dest_ref[2 * t], buf_ref.at[0], r, sem).start()
        _row_copy(yb_ref, dest_ref[2 * t + 1], buf_ref.at[1], r, sem).start()
        return carry

    lax.fori_loop(0, tm, issue, 0)

    def drain(r, carry):
        _row_copy(yb_ref, 0, buf_ref.at[0], 0, sem).wait()
        return carry

    lax.fori_loop(0, 2 * tm, drain, 0)
    w = w_ref[...]
    y = w[:, 0:1] * buf_ref[0] + w[:, 1:2] * buf_ref[1]
    v = alpha * x_ref[...] + gt_ref[0] * y
    o_ref[...] = _ln_rows(v) * g_ref[...] + b_ref[...]


def _combine(dest_flat, yb, w_lanes, x1, gt, g, b, seq, alpha):
    T, D = x1.shape
    tm = _pick(seq, (256, 128))
    per_b = seq // tm
    return pl.pallas_call(
        functools.partial(_combine_kernel, alpha=alpha),
        grid_spec=pltpu.PrefetchScalarGridSpec(
            num_scalar_prefetch=1,
            grid=(T // tm,),
            in_specs=[
                pl.BlockSpec(memory_space=pl.ANY),
                pl.BlockSpec((tm, LANES), lambda i, d: (i, 0)),
                pl.BlockSpec((tm, D), lambda i, d: (i, 0)),
                pl.BlockSpec((1, 1, D), lambda i, d: (i // per_b, 0, 0)),
                pl.BlockSpec((1, D), lambda i, d: (0, 0)),
                pl.BlockSpec((1, D), lambda i, d: (0, 0)),
            ],
            out_specs=pl.BlockSpec((tm, D), lambda i, d: (i, 0)),
            scratch_shapes=[pltpu.VMEM((2, tm, D), F32), pltpu.SemaphoreType.DMA],
        ),
        out_shape=jax.ShapeDtypeStruct((T, D), F32),
        compiler_params=_params(("arbitrary",)),
        name="moe_combine_norm",
    )(dest_flat, yb, w_lanes, x1, gt, g, b)


def kernel(x, c, ada_w, ada_b, w_in, b_forget, conv_w, conv_b, conv_ln_g, conv_ln_b, w_out,
           ln1_g, ln1_b, r1_w, r1_b, r2_w, r2_b, w_gate, w_up, w_down, ln2_g, ln2_b):
    B, S, D = x.shape
    L = ada_w.shape[0]
    T = B * S
    alpha = float((2 * L) ** 0.25)
    d_sb, d_fx, c_cv = D // 4, D // 2, D // 4
    n_fx = d_fx // HEAD_DIM
    n_groups = r1_w.shape[-1]
    epg = r2_w.shape[-1]
    n_experts = n_groups * epg
    n_blocks = (2 * T) // MOE_BLOCK + n_experts
    n_slots = n_blocks * MOE_BLOCK
    qkv_cols = 3 * d_sb + 3 * d_fx
    tq = _pick(S, (256, 128))

    rows = -(-B // SUBLANES) * SUBLANES
    c_pad = jnp.zeros((rows, D), F32).at[:B].set(c)
    mod_all = _ada(c_pad, ada_w, ada_b)

    x2d = x.reshape(T, D)
    for l in range(L):
        mod = mod_all[l, :B]
        sh1, sc1, gt1, sh2, sc2, gt2 = [m.reshape(B, 1, D) for m in jnp.split(mod, 6, axis=-1)]

        h1 = _lnmod(x2d, sc1, sh1, S)
        w_qkv = w_in[l, :, :qkv_cols].astype(BF16)
        f_lo = qkv_cols
        g_lo = qkv_cols + n_fx
        w_rest = jnp.concatenate(
            [w_in[l, :, g_lo:g_lo + 2 * c_cv], w_in[l, :, f_lo:f_lo + n_fx],
             jnp.zeros((D, LANES - n_fx), F32)], axis=1).astype(BF16)
        qkv = _matmul(h1, w_qkv, BF16, "proj_qkv")
        rest = _matmul(h1, w_rest, F32, "proj_glu_forget")

        b_pad = jnp.zeros((1, LANES), F32).at[0, :n_fx].set(b_forget[l])
        cum = _forget_cumsum(rest, b_pad, B, S, (2 * c_cv) // LANES)[:, :n_fx]
        cq = cum.reshape(T, n_fx // 2, 2).transpose(1, 0, 2)
        ck = cum.reshape(B, S // tq, tq, n_fx // 2, 2).transpose(0, 3, 1, 4, 2)

        nb = LANES
        o_sb = _sb_attention(qkv, B, S, d_sb // nb, 0, d_sb // nb, 2 * d_sb // nb)
        fx0 = 3 * d_sb // nb
        o_fx = _fox_attention(qkv, cq, ck, B, S, d_fx // nb, fx0, fx0 + d_fx // nb,
                              fx0 + 2 * d_fx // nb)
        o_cv = _conv_module(rest, conv_w[l], conv_b[l], conv_ln_g[l], conv_ln_b[l], B, S)

        rw = jnp.concatenate(
            [r1_w[l], r2_w[l].transpose(1, 0, 2).reshape(D, n_experts),
             jnp.zeros((D, LANES - n_groups - n_experts), F32)], axis=1)
        rw_hi = rw.astype(BF16)
        rw_lo = (rw - rw_hi.astype(F32)).astype(BF16)
        rb = jnp.zeros((1, LANES), F32).at[0, :n_groups].set(r1_b[l])
        rb = rb.at[0, n_groups:n_groups + n_experts].set(r2_b[l].reshape(-1))
        x1, h2, logits = _outproj(
            o_sb, o_fx, o_cv, w_out[l].astype(BF16), x2d, gt1,
            ln1_g[l].reshape(1, D), ln1_b[l].reshape(1, D), sc2, sh2, rw_hi, rw_lo, rb, S, alpha)

        e_lanes, w_lanes = _route(logits, n_groups, epg)
        dest_lanes, be_lanes = _plan(e_lanes, n_experts, n_blocks)
        dest_flat = dest_lanes[:, :2].reshape(-1)
        be = be_lanes[:n_blocks, 0]
        n_used = jnp.sum((be < n_experts).astype(jnp.int32)).reshape(1)
        block_expert = jnp.minimum(be, n_experts - 1)
        xb = _dispatch(dest_flat, h2, jnp.zeros((n_slots, D), F32))
        yb = _experts(block_expert, n_used, xb, w_gate[l].astype(BF16), w_up[l].astype(BF16),
                      w_down[l].astype(BF16), n_blocks)
        x2d = _combine(dest_flat, yb, w_lanes, x1, gt2,
                       ln2_g[l].reshape(1, D), ln2_b[l].reshape(1, D), S, alpha)
    return x2d.reshape(B, S, D)
```

```python
import functools

import jax
import jax.numpy as jnp
from jax import lax
from jax.experimental import pallas as pl
from jax.experimental.pallas import tpu as pltpu

LN_EPS = 1e-5
HEAD_DIM = 64
LANES = 128
SUBLANES = 8
MOE_BLOCK = 256
NEG_BIG = -1e30
SB_DEAD_LOG = -104.0
VMEM_LIMIT = 56 * 1024 * 1024

F32 = jnp.float32
BF16 = jnp.bfloat16


def _pick(n, cands):
    for c in cands:
        if n % c == 0:
            return c
    return n


def _params(sem):
    return pltpu.CompilerParams(dimension_semantics=sem, vmem_limit_bytes=VMEM_LIMIT)


def _ln_rows(v):
    mu = jnp.mean(v, axis=-1, keepdims=True)
    d = v - mu
    var = jnp.mean(d * d, axis=-1, keepdims=True)
    return d * lax.rsqrt(var + LN_EPS)


def _log_sigmoid(z):
    return jnp.minimum(z, 0.0) - jnp.log1p(jnp.exp(-jnp.abs(z)))


def _sigmoid(z):
    return 1.0 / (1.0 + jnp.exp(-z))


def _split2(v):
    hi = v.astype(BF16)
    lo = (v - hi.astype(F32)).astype(BF16)
    return hi, lo


def _dot(a, b):
    return jnp.dot(a, b, preferred_element_type=F32)


def _dot_nt(a, b):
    return lax.dot_general(a, b, (((1,), (1,)), ((), ())), preferred_element_type=F32)


def _ada_kernel(c_ref, w_ref, b_ref, o_ref):
    c = c_ref[...]
    s = c * _sigmoid(c)
    s_hi, s_lo = _split2(s)
    w_hi, w_lo = _split2(w_ref[0])
    acc = _dot(s_hi, w_hi) + _dot(s_lo, w_hi) + _dot(s_hi, w_lo)
    o_ref[0] = acc + b_ref[0]


def _ada(c_pad, ada_w, ada_b):
    L, D, N = ada_w.shape
    rows = c_pad.shape[0]
    tn = _pick(N, (512, 256, 128))
    return pl.pallas_call(
        _ada_kernel,
        grid=(L, N // tn),
        in_specs=[
            pl.BlockSpec((rows, D), lambda l, n: (0, 0)),
            pl.BlockSpec((1, D, tn), lambda l, n: (l, 0, n)),
            pl.BlockSpec((1, 1, tn), lambda l, n: (l, 0, n)),
        ],
        out_specs=pl.BlockSpec((1, rows, tn), lambda l, n: (l, 0, n)),
        out_shape=jax.ShapeDtypeStruct((L, rows, N), F32),
        compiler_params=_params(("arbitrary", "arbitrary")),
        name="ada_mod",
    )(c_pad, ada_w, ada_b.reshape(L, 1, N))


def _lnmod_kernel(x_ref, sc_ref, sh_ref, o_ref):
    h = _ln_rows(x_ref[...]) * (1.0 + sc_ref[0]) + sh_ref[0]
    o_ref[...] = h.astype(o_ref.dtype)


def _lnmod(x2d, sc, sh, seq):
    T, D = x2d.shape
    tm = _pick(seq, (512, 256, 128))
    per_b = seq // tm
    return pl.pallas_call(
        _lnmod_kernel,
        grid=(T // tm,),
        in_specs=[
            pl.BlockSpec((tm, D), lambda i: (i, 0)),
            pl.BlockSpec((1, 1, D), lambda i: (i // per_b, 0, 0)),
            pl.BlockSpec((1, 1, D), lambda i: (i // per_b, 0, 0)),
        ],
        out_specs=pl.BlockSpec((tm, D), lambda i: (i, 0)),
        out_shape=jax.ShapeDtypeStruct((T, D), BF16),
        compiler_params=_params(("arbitrary",)),
        name="ln_mod",
    )(x2d, sc, sh)


def _mm_kernel(a_ref, b_ref, o_ref):
    o_ref[...] = _dot(a_ref[...], b_ref[...]).astype(o_ref.dtype)


def _matmul(a, w, out_dtype, name):
    M, K = a.shape
    N = w.shape[1]
    tm = _pick(M, (1024, 512, 256, 128))
    tn = _pick(N, (512, 384, 256, 128))
    return pl.pallas_call(
        _mm_kernel,
        grid=(M // tm, N // tn),
        in_specs=[
            pl.BlockSpec((tm, K), lambda i, j: (i, 0)),
            pl.BlockSpec((K, tn), lambda i, j: (0, j)),
        ],
        out_specs=pl.BlockSpec((tm, tn), lambda i, j: (i, j)),
        out_shape=jax.ShapeDtypeStruct((M, N), out_dtype),
        compiler_params=_params(("arbitrary", "arbitrary")),
        name=name,
    )(a, w)


def _cum_kernel(f_ref, b_ref, o_ref, carry_ref):
    @pl.when(pl.program_id(1) == 0)
    def _():
        carry_ref[...] = jnp.zeros_like(carry_ref)

    ts = f_ref.shape[0]
    lf = _log_sigmoid(f_ref[...] + b_ref[...])
    p1 = lf.astype(BF16)
    r1 = lf - p1.astype(F32)
    p2 = r1.astype(BF16)
    p3 = (r1 - p2.astype(F32)).astype(BF16)
    row = lax.broadcasted_iota(jnp.int32, (ts, ts), 0)
    col = lax.broadcasted_iota(jnp.int32, (ts, ts), 1)
    tri = jnp.where(col <= row, 1.0, 0.0).astype(BF16)
    cum = _dot(tri, p1) + _dot(tri, p2) + _dot(tri, p3) + carry_ref[...]
    o_ref[...] = cum
    carry_ref[...] = cum[ts - 1:ts, :]


def _forget_cumsum(rest, b_pad, batch, seq, col_block):
    T = rest.shape[0]
    ts = _pick(seq, (256, 128))
    ns = seq // ts
    return pl.pallas_call(
        _cum_kernel,
        grid=(batch, ns),
        in_specs=[
            pl.BlockSpec((ts, LANES), lambda b, s: (b * ns + s, col_block)),
            pl.BlockSpec((1, LANES), lambda b, s: (0, 0)),
        ],
        out_specs=pl.BlockSpec((ts, LANES), lambda b, s: (b * ns + s, 0)),
        out_shape=jax.ShapeDtypeStruct((T, LANES), F32),
        scratch_shapes=[pltpu.VMEM((1, LANES), F32)],
        compiler_params=_params(("arbitrary", "arbitrary")),
        name="forget_cumsum",
    )(rest, b_pad)


def _sb_kernel(q_ref, k_ref, v_ref, o_ref):
    tq = q_ref.shape[0]
    i = pl.program_id(2)
    lane = lax.broadcasted_iota(jnp.int32, (1, LANES), 1)
    q2 = q_ref[...] * jnp.asarray(HEAD_DIM ** -0.5, q_ref.dtype)
    zero = jnp.zeros_like(q2)
    q_heads = (jnp.where(lane < HEAD_DIM, q2, zero), jnp.where(lane >= HEAD_DIM, q2, zero))
    row = lax.broadcasted_iota(jnp.int32, (tq, tq), 0)
    col = lax.broadcasted_iota(jnp.int32, (tq, tq), 1)
    suffix = jnp.where(row > col, 1.0, 0.0).astype(BF16)

    def block(j, state, diagonal):
        start = pl.multiple_of(j * tq, tq)
        kblk = k_ref[pl.ds(start, tq), :]
        vblk = v_ref[pl.ds(start, tq), :]
        out = []
        for h in range(2):
            run, acc = state[2 * h], state[2 * h + 1]
            z = _dot_nt(q_heads[h], kblk)
            ls = _log_sigmoid(z)
            lk = ls - z
            if diagonal:
                lk = jnp.where(col < row, lk, 0.0)
            lk_hi, lk_lo = _split2(lk)
            later = _dot(lk_hi, suffix) + _dot(lk_lo, suffix)
            w = jnp.exp(ls + later + run)
            if diagonal:
                w = jnp.where(col < row, w, 0.0)
            acc = acc + _dot(w.astype(BF16), vblk)
            run = run + jnp.sum(lk, axis=1, keepdims=True)
            out += [run, acc]
        return tuple(out)

    def live(state):
        return (jnp.max(jnp.maximum(state[0], state[2])) > SB_DEAD_LOG).astype(jnp.int32)

    init = (jnp.zeros((tq, 1), F32), jnp.zeros((tq, LANES), F32)) * 2
    state = block(i, init, True)

    def cond(carry):
        return (carry[0] >= 0) & (carry[1] > 0)

    def body(carry):
        state = block(carry[0], carry[2:], False)
        return (carry[0] - 1, live(state)) + state

    res = lax.while_loop(cond, body, (i - 1, live(state)) + state)
    o_ref[...] = jnp.where(lane < HEAD_DIM, res[3], res[5]).astype(o_ref.dtype)


def _sb_attention(qkv, batch, seq, n_pairs, q_col, k_col, v_col):
    T = qkv.shape[0]
    tq = _pick(seq, (256, 128))
    nq = seq // tq
    return pl.pallas_call(
        _sb_kernel,
        grid=(batch, n_pairs, nq),
        in_specs=[
            pl.BlockSpec((tq, LANES), lambda b, p, i: (b * nq + i, q_col + p)),
            pl.BlockSpec((seq, LANES), lambda b, p, i: (b, k_col + p)),
            pl.BlockSpec((seq, LANES), lambda b, p, i: (b, v_col + p)),
        ],
        out_specs=pl.BlockSpec((tq, LANES), lambda b, p, i: (b * nq + i, p)),
        out_shape=jax.ShapeDtypeStruct((T, n_pairs * LANES), BF16),
        compiler_params=_params(("arbitrary", "arbitrary", "arbitrary")),
        name="sb_attention",
    )(qkv, qkv, qkv)


def _fox_kernel(q_ref, k_ref, v_ref, cq_ref, ck_ref, o_ref):
    tq = q_ref.shape[0]
    i = pl.program_id(2)
    lane = lax.broadcasted_iota(jnp.int32, (1, LANES), 1)
    q2 = q_ref[...] * jnp.asarray(HEAD_DIM ** -0.5, q_ref.dtype)
    zero = jnp.zeros_like(q2)
    q_heads = (jnp.where(lane < HEAD_DIM, q2, zero), jnp.where(lane >= HEAD_DIM, q2, zero))
    cq = cq_ref[0]
    cq_heads = (cq[:, 0:1], cq[:, 1:2])
    row = lax.broadcasted_iota(jnp.int32, (tq, tq), 0)
    col = lax.broadcasted_iota(jnp.int32, (tq, tq), 1)

    def block(j, carry, diagonal):
        start = pl.multiple_of(j * tq, tq)
        kblk = k_ref[pl.ds(start, tq), :]
        vblk = v_ref[pl.ds(start, tq), :]
        ck = ck_ref[0, 0, j]
        out = []
        for h in range(2):
            m, l, acc = carry[3 * h], carry[3 * h + 1], carry[3 * h + 2]
            logits = _dot_nt(q_heads[h], kblk) + cq_heads[h] - ck[h:h + 1, :]
            if diagonal:
                logits = jnp.where(col <= row, logits, NEG_BIG)
            m_new = jnp.maximum(m, jnp.max(logits, axis=1, keepdims=True))
            alpha = jnp.exp(m - m_new)
            p = jnp.exp(logits - m_new)
            l = l * alpha + jnp.sum(p, axis=1, keepdims=True)
            acc = acc * alpha + _dot(p.astype(BF16), vblk)
            out += [m_new, l, acc]
        return tuple(out)

    init = (jnp.full((tq, 1), NEG_BIG, F32), jnp.zeros((tq, 1), F32),
            jnp.zeros((tq, LANES), F32)) * 2
    state = block(i, init, True)
    res = lax.fori_loop(0, i, lambda j, c: block(j, c, False), state)
    o = jnp.where(lane < HEAD_DIM, res[2] / res[1], res[5] / res[4])
    o_ref[...] = o.astype(o_ref.dtype)


def _fox_attention(qkv, cq, ck, batch, seq, n_pairs, q_col, k_col, v_col):
    T = qkv.shape[0]
    tq = ck.shape[-1]
    nq = seq // tq
    return pl.pallas_call(
        _fox_kernel,
        grid=(batch, n_pairs, nq),
        in_specs=[
            pl.BlockSpec((tq, LANES), lambda b, p, i: (b * nq + i, q_col + p)),
            pl.BlockSpec((seq, LANES), lambda b, p, i: (b, k_col + p)),
            pl.BlockSpec((seq, LANES), lambda b, p, i: (b, v_col + p)),
            pl.BlockSpec((1, tq, 2), lambda b, p, i: (p, b * nq + i, 0)),
            pl.BlockSpec((1, 1, nq, 2, tq), lambda b, p, i: (b, p, 0, 0, 0)),
        ],
        out_specs=pl.BlockSpec((tq, LANES), lambda b, p, i: (b * nq + i, p)),
        out_shape=jax.ShapeDtypeStruct((T, n_pairs * LANES), BF16),
        compiler_params=_params(("arbitrary", "arbitrary", "arbitrary")),
        name="fox_attention",
    )(qkv, qkv, qkv, cq, ck)


def _conv_kernel(a_ref, g_ref, w_ref, cb_ref, lg_ref, lb_ref, o_ref, u_ref, *, width, halo):
    ts = a_ref.shape[0]

    @pl.when(pl.program_id(1) == 0)
    def _():
        u_ref[0:halo, :] = jnp.zeros((halo, u_ref.shape[1]), F32)

    u_ref[halo:halo + ts, :] = a_ref[...] * _sigmoid(g_ref[...])
    acc = jnp.zeros(a_ref.shape, F32) + cb_ref[...]
    for k in range(width):
        off = halo - (width - 1) + k
        acc = acc + w_ref[k:k + 1, :] * u_ref[off:off + ts, :]
    y = _ln_rows(acc) * lg_ref[...] + lb_ref[...]
    o_ref[...] = (y * _sigmoid(y)).astype(o_ref.dtype)
    u_ref[0:halo, :] = u_ref[ts:ts + halo, :]


def _conv_module(rest, conv_w, conv_b, ln_g, ln_b, batch, seq):
    T = rest.shape[0]
    width, C = conv_w.shape
    halo = -(-(width - 1) // SUBLANES) * SUBLANES
    ts = _pick(seq, (128,))
    ns = seq // ts
    kern = functools.partial(_conv_kernel, width=width, halo=halo)
    vec = lambda: pl.BlockSpec((1, C), lambda b, s: (0, 0))
    return pl.pallas_call(
        kern,
        grid=(batch, ns),
        in_specs=[
            pl.BlockSpec((ts, C), lambda b, s: (b * ns + s, 0)),
            pl.BlockSpec((ts, C), lambda b, s: (b * ns + s, 1)),
            pl.BlockSpec((width, C), lambda b, s: (0, 0)),
            vec(), vec(), vec(),
        ],
        out_specs=pl.BlockSpec((ts, C), lambda b, s: (b * ns + s, 0)),
        out_shape=jax.ShapeDtypeStruct((T, C), BF16),
        scratch_shapes=[pltpu.VMEM((ts + halo, C), F32)],
        compiler_params=_params(("arbitrary", "arbitrary")),
        name="conformer_conv",
    )(rest, rest, conv_w, conv_b.reshape(1, C), ln_g.reshape(1, C), ln_b.reshape(1, C))


def _outproj_kernel(osb_ref, ofx_ref, ocv_ref, w_ref, x_ref, gt_ref, g_ref, b_ref,
                    sc_ref, sh_ref, rwh_ref, rwl_ref, rb_ref,
                    x1_ref, h2_ref, lg_ref, *, alpha):
    d_sb = osb_ref.shape[1]
    d_fx = ofx_ref.shape[1]
    y = _dot(osb_ref[...], w_ref[0:d_sb, :])
    y = y + _dot(ofx_ref[...], w_ref[d_sb:d_sb + d_fx, :])
    y = y + _dot(ocv_ref[...], w_ref[d_sb + d_fx:, :])
    x1 = _ln_rows(alpha * x_ref[...] + gt_ref[0] * y) * g_ref[...] + b_ref[...]
    x1_ref[...] = x1
    h2 = _ln_rows(x1) * (1.0 + sc_ref[0]) + sh_ref[0]
    h2_ref[...] = h2
    h_hi, h_lo = _split2(h2)
    lg = _dot(h_hi, rwh_ref[...]) + _dot(h_lo, rwh_ref[...]) + _dot(h_hi, rwl_ref[...])
    lg_ref[...] = lg + rb_ref[...]


def _outproj(o_sb, o_fx, o_cv, w_out, x2d, gt, g, b, sc2, sh2, rw_hi, rw_lo, rb, seq, alpha):
    T, D = x2d.shape
    tm = _pick(seq, (256, 128))
    per_b = seq // tm
    rowblk = lambda n: pl.BlockSpec((tm, n), lambda i: (i, 0))
    full = lambda r, c: pl.BlockSpec((r, c), lambda i: (0, 0))
    perb = lambda: pl.BlockSpec((1, 1, D), lambda i: (i // per_b, 0, 0))
    return pl.pallas_call(
        functools.partial(_outproj_kernel, alpha=alpha),
        grid=(T // tm,),
        in_specs=[
            rowblk(o_sb.shape[1]), rowblk(o_fx.shape[1]), rowblk(o_cv.shape[1]),
            full(D, D), rowblk(D), perb(), full(1, D), full(1, D), perb(), perb(),
            full(D, LANES), full(D, LANES), full(1, LANES),
        ],
        out_specs=[rowblk(D), rowblk(D), rowblk(LANES)],
        out_shape=[jax.ShapeDtypeStruct((T, D), F32), jax.ShapeDtypeStruct((T, D), F32),
                   jax.ShapeDtypeStruct((T, LANES), F32)],
        compiler_params=_params(("arbitrary",)),
        name="out_proj_norm_router",
    )(o_sb, o_fx, o_cv, w_out, x2d, gt, g, b, sc2, sh2, rw_hi, rw_lo, rb)


def _route_kernel(lg_ref, e_ref, w_ref, *, n_groups, epg):
    L = lg_ref[...]
    lane_i = lax.broadcasted_iota(jnp.int32, L.shape, 1)
    lane = lane_i.astype(F32)
    far = 1e6
    is_grp = lane < n_groups
    l1 = jnp.where(is_grp, L, NEG_BIG)
    m1 = jnp.max(l1, axis=1, keepdims=True)
    grp = jnp.min(jnp.where(is_grp & (l1 == m1), lane, far), axis=1, keepdims=True)
    s1 = jnp.sum(jnp.where(is_grp, jnp.exp(l1 - m1), 0.0), axis=1, keepdims=True)
    p_grp = 1.0 / s1
    lo = n_groups + grp * epg
    in_grp = (lane >= lo) & (lane < lo + epg)
    l2 = jnp.where(in_grp, L, NEG_BIG)
    m2 = jnp.max(l2, axis=1, keepdims=True)
    e2 = jnp.where(in_grp, jnp.exp(l2 - m2), -1.0)
    v1 = jnp.max(e2, axis=1, keepdims=True)
    i1 = jnp.min(jnp.where(e2 == v1, lane, far), axis=1, keepdims=True)
    e2b = jnp.where(lane == i1, -1.0, e2)
    v2 = jnp.max(e2b, axis=1, keepdims=True)
    i2 = jnp.min(jnp.where(e2b == v2, lane, far), axis=1, keepdims=True)
    den = v1 + v2
    w0 = p_grp * v1 / den
    w1 = p_grp * v2 / den
    ids = jnp.where(lane_i == 0, i1 - n_groups, jnp.where(lane_i == 1, i2 - n_groups, 0.0))
    e_ref[...] = ids.astype(jnp.int32)
    w_ref[...] = jnp.where(lane_i == 0, w0, jnp.where(lane_i == 1, w1, 0.0))


def _route(logits, n_groups, epg):
    T = logits.shape[0]
    tm = _pick(T, (512, 256, 128))
    blk = lambda: pl.BlockSpec((tm, LANES), lambda i: (i, 0))
    return pl.pallas_call(
        functools.partial(_route_kernel, n_groups=n_groups, epg=epg),
        grid=(T // tm,),
        in_specs=[blk()],
        out_specs=[blk(), blk()],
        out_shape=[jax.ShapeDtypeStruct((T, LANES), jnp.int32),
                   jax.ShapeDtypeStruct((T, LANES), F32)],
        compiler_params=_params(("arbitrary",)),
        name="route_topk",
    )(logits)


def _plan_kernel(e_ref, dest_ref, be_ref, cnt_ref, base_ref, *, n_experts):
    p = pl.program_id(0)
    i = pl.program_id(1)
    tm = e_ref.shape[0]
    lane = lax.broadcasted_iota(jnp.int32, (tm, LANES), 1)
    e = e_ref[...]
    hot0 = lane == e[:, 0:1]
    hot1 = lane == e[:, 1:2]
    both = jnp.where(hot0 | hot1, 1.0, 0.0)

    @pl.when((p == 0) & (i == 0))
    def _():
        cnt_ref[...] = jnp.zeros_like(cnt_ref)

    @pl.when(p == 0)
    def _():
        cnt_ref[...] += jnp.sum(both, axis=0, keepdims=True)

    @pl.when((p == 1) & (i == 0))
    def _():
        nblk = jnp.floor((cnt_ref[...] + (MOE_BLOCK - 1)) * (1.0 / MOE_BLOCK))
        r = lax.broadcasted_iota(jnp.int32, (LANES, LANES), 0)
        c = lax.broadcasted_iota(jnp.int32, (LANES, LANES), 1)
        before = jnp.where(r < c, 1.0, 0.0).astype(BF16)
        nb8 = jnp.broadcast_to(nblk, (SUBLANES, LANES)).astype(BF16)
        excl = _dot(nb8, before)[0:1, :]
        base_ref[...] = excl * MOE_BLOCK
        cnt_ref[...] = jnp.zeros_like(cnt_ref)
        incl = excl + nblk
        nb_rows = be_ref.shape[0]
        bidx = lax.broadcasted_iota(jnp.int32, (nb_rows, LANES), 0).astype(F32)
        lane_b = lax.broadcasted_iota(jnp.int32, (nb_rows, LANES), 1)
        done = jnp.where((incl <= bidx) & (lane_b < n_experts), 1.0, 0.0)
        be = jnp.sum(done, axis=1, keepdims=True)
        be_ref[...] = jnp.broadcast_to(be, (nb_rows, LANES)).astype(jnp.int32)

    @pl.when(p == 1)
    def _():
        r = lax.broadcasted_iota(jnp.int32, (tm, tm), 0)
        c = lax.broadcasted_iota(jnp.int32, (tm, tm), 1)
        earlier = jnp.where(c < r, 1.0, 0.0).astype(BF16)
        tot = _dot(earlier, both.astype(BF16)) + cnt_ref[...] + base_ref[...]
        d0 = jnp.sum(jnp.where(hot0, tot, 0.0), axis=1, keepdims=True)
        d1 = jnp.sum(jnp.where(hot1, tot, 0.0), axis=1, keepdims=True)
        dest = jnp.where(lane == 0, d0, jnp.where(lane == 1, d1, 0.0))
        dest_ref[...] = dest.astype(jnp.int32)
        cnt_ref[...] += jnp.sum(both, axis=0, keepdims=True)


def _plan(e_lanes, n_experts, n_blocks):
    T = e_lanes.shape[0]
    tm = _pick(T, (256, 128))
    nb_rows = -(-n_blocks // SUBLANES) * SUBLANES
    return pl.pallas_call(
        functools.partial(_plan_kernel, n_experts=n_experts),
        grid=(2, T // tm),
        in_specs=[pl.BlockSpec((tm, LANES), lambda p, i: (i, 0))],
        out_specs=[pl.BlockSpec((tm, LANES), lambda p, i: (i * p, 0)),
                   pl.BlockSpec((nb_rows, LANES), lambda p, i: (0, 0))],
        out_shape=[jax.ShapeDtypeStruct((T, LANES), jnp.int32),
                   jax.ShapeDtypeStruct((nb_rows, LANES), jnp.int32)],
        scratch_shapes=[pltpu.VMEM((1, LANES), F32), pltpu.VMEM((1, LANES), F32)],
        compiler_params=_params(("arbitrary", "arbitrary")),
        name="dispatch_plan",
    )(e_lanes)


def _row_copy(src, src_row, dst, dst_row, sem):
    return pltpu.make_async_copy(src.at[pl.ds(src_row, 1)], dst.at[pl.ds(dst_row, 1)], sem)


def _slotmap_kernel(dest_ref, tok_ref):
    def clear(s, carry):
        tok_ref[s] = 0
        return carry

    lax.fori_loop(0, tok_ref.shape[0], clear, 0, unroll=8)

    def put(t, carry):
        tok_ref[dest_ref[2 * t]] = t
        tok_ref[dest_ref[2 * t + 1]] = t
        return carry

    lax.fori_loop(0, dest_ref.shape[0] // 2, put, 0, unroll=8)


def _slot_map(dest_flat, n_slots):
    return pl.pallas_call(
        _slotmap_kernel,
        in_specs=[pl.BlockSpec(memory_space=pltpu.SMEM)],
        out_specs=pl.BlockSpec(memory_space=pltpu.SMEM),
        out_shape=jax.ShapeDtypeStruct((n_slots,), jnp.int32),
        name="moe_slot_map",
    )(dest_flat)


def _expert_kernel(be_ref, nu_ref, tok_ref, h_ref, wg_ref, wu_ref, wd_ref, o_ref, xbuf, sems):
    del be_ref
    b = pl.program_id(0)
    n_used = nu_ref[0]
    slot = b % 2

    def gather(blk, s):
        def issue(r, carry):
            _row_copy(h_ref, tok_ref[blk * MOE_BLOCK + r], xbuf.at[s], r, sems.at[s]).start()
            return carry

        lax.fori_loop(0, MOE_BLOCK, issue, 0, unroll=8)

    @pl.when(b == 0)
    def _():
        gather(0, 0)

    @pl.when(b + 1 < n_used)
    def _():
        gather(b + 1, 1 - slot)

    @pl.when(b < n_used)
    def _():
        def drain(r, carry):
            _row_copy(h_ref, 0, xbuf.at[slot], 0, sems.at[slot]).wait()
            return carry

        lax.fori_loop(0, MOE_BLOCK, drain, 0, unroll=8)
        x = xbuf[slot].astype(BF16)
        g = _dot(x, wg_ref[0])
        u = _dot(x, wu_ref[0])
        hid = (g * _sigmoid(g)) * u
        o_ref[...] = _dot(hid.astype(BF16), wd_ref[0])

    @pl.when(b >= n_used)
    def _():
        o_ref[...] = jnp.zeros_like(o_ref)


def _experts(block_expert, n_used, slot_tok, h2, w_gate, w_up, w_down, n_blocks):
    D = h2.shape[1]
    DE = w_gate.shape[2]
    wmap = lambda b, be, nu, tok: (be[jnp.minimum(b, nu[0] - 1)], 0, 0)
    return pl.pallas_call(
        _expert_kernel,
        grid_spec=pltpu.PrefetchScalarGridSpec(
            num_scalar_prefetch=3,
            grid=(n_blocks,),
            in_specs=[
                pl.BlockSpec(memory_space=pl.ANY),
                pl.BlockSpec((1, D, DE), wmap),
                pl.BlockSpec((1, D, DE), wmap),
                pl.BlockSpec((1, DE, D), wmap),
            ],
            out_specs=pl.BlockSpec((MOE_BLOCK, D), lambda b, be, nu, tok: (b, 0)),
            scratch_shapes=[pltpu.VMEM((2, MOE_BLOCK, D), F32), pltpu.SemaphoreType.DMA((2,))],
        ),
        out_shape=jax.ShapeDtypeStruct((n_blocks * MOE_BLOCK, D), F32),
        compiler_params=_params(("arbitrary",)),
        name="moe_experts",
    )(block_expert, n_used, slot_tok, h2, w_gate, w_up, w_down)


def _combine_kernel(dest_ref, yb_ref, w_ref, x_ref, gt_ref, g_ref, b_ref, o_ref,
                    buf_ref, sem, *, alpha):
    tm = x_ref.shape[0]
    base = pl.program_id(0) * tm

    def issue(r, carry):
        t = base + r
        _row_copy(yb_ref, dest_ref[2 * t], buf_ref.at[0], r, sem).start()
        _row_copy(yb_ref, dest_ref[2 * t + 1], buf_ref.at[1], r, sem).start()
        return carry

    lax.fori_loop(0, tm, issue, 0)

    def drain(r, carry):
        _row_copy(yb_ref, 0, buf_ref.at[0], 0, sem).wait()
        return carry

    lax.fori_loop(0, 2 * tm, drain, 0)
    w = w_ref[...]
    y = w[:, 0:1] * buf_ref[0] + w[:, 1:2] * buf_ref[1]
    v = alpha * x_ref[...] + gt_ref[0] * y
    o_ref[...] = _ln_rows(v) * g_ref[...] + b_ref[...]


def _combine(dest_flat, yb, w_lanes, x1, gt, g, b, seq, alpha):
    T, D = x1.shape
    tm = _pick(seq, (256, 128))
    per_b = seq // tm
    return pl.pallas_call(
        functools.partial(_combine_kernel, alpha=alpha),
        grid_spec=pltpu.PrefetchScalarGridSpec(
            num_scalar_prefetch=1,
            grid=(T // tm,),
            in_specs=[
                pl.BlockSpec(memory_space=pl.ANY),
                pl.BlockSpec((tm, LANES), lambda i, d: (i, 0)),
                pl.BlockSpec((tm, D), lambda i, d: (i, 0)),
                pl.BlockSpec((1, 1, D), lambda i, d: (i // per_b, 0, 0)),
                pl.BlockSpec((1, D), lambda i, d: (0, 0)),
                pl.BlockSpec((1, D), lambda i, d: (0, 0)),
            ],
            out_specs=pl.BlockSpec((tm, D), lambda i, d: (i, 0)),
            scratch_shapes=[pltpu.VMEM((2, tm, D), F32), pltpu.SemaphoreType.DMA],
        ),
        out_shape=jax.ShapeDtypeStruct((T, D), F32),
        compiler_params=_params(("arbitrary",)),
        name="moe_combine_norm",
    )(dest_flat, yb, w_lanes, x1, gt, g, b)


def kernel(x, c, ada_w, ada_b, w_in, b_forget, conv_w, conv_b, conv_ln_g, conv_ln_b, w_out,
           ln1_g, ln1_b, r1_w, r1_b, r2_w, r2_b, w_gate, w_up, w_down, ln2_g, ln2_b):
    B, S, D = x.shape
    L = ada_w.shape[0]
    T = B * S
    alpha = float((2 * L) ** 0.25)
    d_sb, d_fx, c_cv = D // 4, D // 2, D // 4
    n_fx = d_fx // HEAD_DIM
    n_groups = r1_w.shape[-1]
    epg = r2_w.shape[-1]
    n_experts = n_groups * epg
    n_blocks = (2 * T) // MOE_BLOCK + n_experts
    n_slots = n_blocks * MOE_BLOCK
    qkv_cols = 3 * d_sb + 3 * d_fx
    tq = _pick(S, (512, 256, 128))

    rows = -(-B // SUBLANES) * SUBLANES
    c_pad = jnp.zeros((rows, D), F32).at[:B].set(c)
    mod_all = _ada(c_pad, ada_w, ada_b)

    x2d = x.reshape(T, D)
    for l in range(L):
        mod = mod_all[l, :B]
        sh1, sc1, gt1, sh2, sc2, gt2 = [m.reshape(B, 1, D) for m in jnp.split(mod, 6, axis=-1)]

        h1 = _lnmod(x2d, sc1, sh1, S)
        w_qkv = w_in[l, :, :qkv_cols].astype(BF16)
        f_lo = qkv_cols
        g_lo = qkv_cols + n_fx
        w_rest = jnp.concatenate(
            [w_in[l, :, g_lo:g_lo + 2 * c_cv], w_in[l, :, f_lo:f_lo + n_fx],
             jnp.zeros((D, LANES - n_fx), F32)], axis=1).astype(BF16)
        qkv = _matmul(h1, w_qkv, BF16, "proj_qkv")
        rest = _matmul(h1, w_rest, F32, "proj_glu_forget")

        b_pad = jnp.zeros((1, LANES), F32).at[0, :n_fx].set(b_forget[l])
        cum = _forget_cumsum(rest, b_pad, B, S, (2 * c_cv) // LANES)[:, :n_fx]
        cq = cum.reshape(T, n_fx // 2, 2).transpose(1, 0, 2)
        ck = cum.reshape(B, S // tq, tq, n_fx // 2, 2).transpose(0, 3, 1, 4, 2)

        nb = LANES
        o_sb = _sb_attention(qkv, B, S, d_sb // nb, 0, d_sb // nb, 2 * d_sb // nb)
        fx0 = 3 * d_sb // nb
        o_fx = _fox_attention(qkv, cq, ck, B, S, d_fx // nb, fx0, fx0 + d_fx // nb,
                              fx0 + 2 * d_fx // nb)
        o_cv = _conv_module(rest, conv_w[l], conv_b[l], conv_ln_g[l], conv_ln_b[l], B, S)

        rw = jnp.concatenate(
            [r1_w[l], r2_w[l].transpose(1, 0, 2).reshape(D, n_experts),
             jnp.zeros((D, LANES - n_groups - n_experts), F32)], axis=1)
        rw_hi = rw.astype(BF16)
        rw_lo = (rw - rw_hi.astype(F32)).astype(BF16)
        rb = jnp.zeros((1, LANES), F32).at[0, :n_groups].set(r1_b[l])
        rb = rb.at[0, n_groups:n_groups + n_experts].set(r2_b[l].reshape(-1))
        x1, h2, logits = _outproj(
            o_sb, o_fx, o_cv, w_out[l].astype(BF16), x2d, gt1,
            ln1_g[l].reshape(1, D), ln1_b[l].reshape(1, D), sc2, sh2, rw_hi, rw_lo, rb, S, alpha)

        e_lanes, w_lanes = _route(logits, n_groups, epg)
        dest_lanes, be_lanes = _plan(e_lanes, n_experts, n_blocks)
        dest_flat = dest_lanes[:, :2].reshape(-1)
        be = be_lanes[:n_blocks, 0]
        n_used = jnp.sum((be < n_experts).astype(jnp.int32)).reshape(1)
        block_expert = jnp.minimum(be, n_experts - 1)
        slot_tok = _slot_map(dest_flat, n_slots)
        yb = _experts(block_expert, n_used, slot_tok, h2, w_gate[l].astype(BF16),
                      w_up[l].astype(BF16), w_down[l].astype(BF16), n_blocks)
        x2d = _combine(dest_flat, yb, w_lanes, x1, gt2,
                       ln2_g[l].reshape(1, D), ln2_b[l].reshape(1, D), S, alpha)
    return x2d.reshape(B, S, D)
```

```python
import functools

import jax
import jax.numpy as jnp
from jax import lax
from jax.experimental import pallas as pl
from jax.experimental.pallas import tpu as pltpu

LN_EPS = 1e-5
HEAD_DIM = 64
LANES = 128
SUBLANES = 8
MOE_BLOCK = 256
NEG_BIG = -1e30
SB_DEAD_LOG = -104.0
FOX_PAIRS = 2
VMEM_LIMIT = 56 * 1024 * 1024

F32 = jnp.float32
BF16 = jnp.bfloat16


def _pick(n, cands):
    for c in cands:
        if n % c == 0:
            return c
    return n


def _params(sem):
    return pltpu.CompilerParams(dimension_semantics=sem, vmem_limit_bytes=VMEM_LIMIT)


def _ln_rows(v):
    mu = jnp.mean(v, axis=-1, keepdims=True)
    d = v - mu
    var = jnp.mean(d * d, axis=-1, keepdims=True)
    return d * lax.rsqrt(var + LN_EPS)


def _log_sigmoid(z):
    return jnp.minimum(z, 0.0) - jnp.log1p(jnp.exp(-jnp.abs(z)))


def _sigmoid(z):
    return 1.0 / (1.0 + jnp.exp(-z))


def _split2(v):
    hi = v.astype(BF16)
    lo = (v - hi.astype(F32)).astype(BF16)
    return hi, lo


def _dot(a, b):
    return jnp.dot(a, b, preferred_element_type=F32)


def _dot_nt(a, b):
    return lax.dot_general(a, b, (((1,), (1,)), ((), ())), preferred_element_type=F32)


def _ada_kernel(c_ref, w_ref, b_ref, o_ref):
    c = c_ref[...]
    s = c * _sigmoid(c)
    s_hi, s_lo = _split2(s)
    w_hi, w_lo = _split2(w_ref[0])
    acc = _dot(s_hi, w_hi) + _dot(s_lo, w_hi) + _dot(s_hi, w_lo)
    o_ref[0] = acc + b_ref[0]


def _ada(c_pad, ada_w, ada_b):
    L, D, N = ada_w.shape
    rows = c_pad.shape[0]
    tn = _pick(N, (512, 256, 128))
    return pl.pallas_call(
        _ada_kernel,
        grid=(L, N // tn),
        in_specs=[
            pl.BlockSpec((rows, D), lambda l, n: (0, 0)),
            pl.BlockSpec((1, D, tn), lambda l, n: (l, 0, n)),
            pl.BlockSpec((1, 1, tn), lambda l, n: (l, 0, n)),
        ],
        out_specs=pl.BlockSpec((1, rows, tn), lambda l, n: (l, 0, n)),
        out_shape=jax.ShapeDtypeStruct((L, rows, N), F32),
        compiler_params=_params(("arbitrary", "arbitrary")),
        name="ada_mod",
    )(c_pad, ada_w, ada_b.reshape(L, 1, N))


def _lnmod_kernel(x_ref, sc_ref, sh_ref, o_ref):
    h = _ln_rows(x_ref[...]) * (1.0 + sc_ref[0]) + sh_ref[0]
    o_ref[...] = h.astype(o_ref.dtype)


def _lnmod(x2d, sc, sh, seq):
    T, D = x2d.shape
    tm = _pick(seq, (512, 256, 128))
    per_b = seq // tm
    return pl.pallas_call(
        _lnmod_kernel,
        grid=(T // tm,),
        in_specs=[
            pl.BlockSpec((tm, D), lambda i: (i, 0)),
            pl.BlockSpec((1, 1, D), lambda i: (i // per_b, 0, 0)),
            pl.BlockSpec((1, 1, D), lambda i: (i // per_b, 0, 0)),
        ],
        out_specs=pl.BlockSpec((tm, D), lambda i: (i, 0)),
        out_shape=jax.ShapeDtypeStruct((T, D), BF16),
        compiler_params=_params(("arbitrary",)),
        name="ln_mod",
    )(x2d, sc, sh)


def _mm_kernel(a_ref, b_ref, o_ref):
    o_ref[...] = _dot(a_ref[...], b_ref[...]).astype(o_ref.dtype)


def _matmul(a, w, out_dtype, name):
    M, K = a.shape
    N = w.shape[1]
    tm = _pick(M, (1024, 512, 256, 128))
    tn = _pick(N, (512, 384, 256, 128))
    return pl.pallas_call(
        _mm_kernel,
        grid=(M // tm, N // tn),
        in_specs=[
            pl.BlockSpec((tm, K), lambda i, j: (i, 0)),
            pl.BlockSpec((K, tn), lambda i, j: (0, j)),
        ],
        out_specs=pl.BlockSpec((tm, tn), lambda i, j: (i, j)),
        out_shape=jax.ShapeDtypeStruct((M, N), out_dtype),
        compiler_params=_params(("arbitrary", "arbitrary")),
        name=name,
    )(a, w)


def _cum_kernel(f_ref, b_ref, o_ref, carry_ref):
    @pl.when(pl.program_id(1) == 0)
    def _():
        carry_ref[...] = jnp.zeros_like(carry_ref)

    ts = f_ref.shape[0]
    lf = _log_sigmoid(f_ref[...] + b_ref[...])
    p1 = lf.astype(BF16)
    r1 = lf - p1.astype(F32)
    p2 = r1.astype(BF16)
    p3 = (r1 - p2.astype(F32)).astype(BF16)
    row = lax.broadcasted_iota(jnp.int32, (ts, ts), 0)
    col = lax.broadcasted_iota(jnp.int32, (ts, ts), 1)
    tri = jnp.where(col <= row, 1.0, 0.0).astype(BF16)
    cum = _dot(tri, p1) + _dot(tri, p2) + _dot(tri, p3) + carry_ref[...]
    o_ref[...] = cum
    carry_ref[...] = cum[ts - 1:ts, :]


def _forget_cumsum(rest, b_pad, batch, seq, col_block):
    T = rest.shape[0]
    ts = _pick(seq, (256, 128))
    ns = seq // ts
    return pl.pallas_call(
        _cum_kernel,
        grid=(batch, ns),
        in_specs=[
            pl.BlockSpec((ts, LANES), lambda b, s: (b * ns + s, col_block)),
            pl.BlockSpec((1, LANES), lambda b, s: (0, 0)),
        ],
        out_specs=pl.BlockSpec((ts, LANES), lambda b, s: (b * ns + s, 0)),
        out_shape=jax.ShapeDtypeStruct((T, LANES), F32),
        scratch_shapes=[pltpu.VMEM((1, LANES), F32)],
        compiler_params=_params(("arbitrary", "arbitrary")),
        name="forget_cumsum",
    )(rest, b_pad)


def _masked_heads(q_ref, pairs):
    lane = lax.broadcasted_iota(jnp.int32, (1, LANES), 1)
    heads = []
    for p in range(pairs):
        q2 = q_ref[:, p * LANES:(p + 1) * LANES] * jnp.asarray(HEAD_DIM ** -0.5, q_ref.dtype)
        zero = jnp.zeros_like(q2)
        heads += [jnp.where(lane < HEAD_DIM, q2, zero), jnp.where(lane >= HEAD_DIM, q2, zero)]
    return heads


def _pairs_per_step(cands, *counts):
    return next(c for c in cands if all(n % c == 0 for n in counts))


def _merge_heads(outs):
    lane = lax.broadcasted_iota(jnp.int32, (1, LANES), 1)
    slabs = [jnp.where(lane < HEAD_DIM, outs[2 * p], outs[2 * p + 1]) for p in range(len(outs) // 2)]
    return slabs[0] if len(slabs) == 1 else jnp.concatenate(slabs, axis=1)


def _sb_kernel(q_ref, k_ref, v_ref, o_ref, *, pairs):
    tq = q_ref.shape[0]
    i = pl.program_id(2)
    n_heads = 2 * pairs
    q_heads = _masked_heads(q_ref, pairs)
    row = lax.broadcasted_iota(jnp.int32, (tq, tq), 0)
    col = lax.broadcasted_iota(jnp.int32, (tq, tq), 1)
    suffix = jnp.where(row > col, 1.0, 0.0).astype(BF16)

    def block(j, state, diagonal):
        start = pl.multiple_of(j * tq, tq)
        out = []
        for h in range(n_heads):
            lanes = slice((h // 2) * LANES, (h // 2 + 1) * LANES)
            run, acc = state[2 * h], state[2 * h + 1]
            z = _dot_nt(q_heads[h], k_ref[pl.ds(start, tq), lanes])
            ls = _log_sigmoid(z)
            lk = ls - z
            if diagonal:
                lk = jnp.where(col < row, lk, 0.0)
            lk_hi, lk_lo = _split2(lk)
            later = _dot(lk_hi, suffix) + _dot(lk_lo, suffix)
            w = jnp.exp(ls + later + run)
            if diagonal:
                w = jnp.where(col < row, w, 0.0)
            acc = acc + _dot(w.astype(BF16), v_ref[pl.ds(start, tq), lanes])
            run = run + jnp.sum(lk, axis=1, keepdims=True)
            out += [run, acc]
        return tuple(out)

    def live(state):
        top = state[0]
        for h in range(1, n_heads):
            top = jnp.maximum(top, state[2 * h])
        return (jnp.max(top) > SB_DEAD_LOG).astype(jnp.int32)

    init = (jnp.zeros((tq, 1), F32), jnp.zeros((tq, LANES), F32)) * n_heads
    state = block(i, init, True)

    def cond(carry):
        return (carry[0] >= 0) & (carry[1] > 0)

    def body(carry):
        state = block(carry[0], carry[2:], False)
        return (carry[0] - 1, live(state)) + state

    res = lax.while_loop(cond, body, (i - 1, live(state)) + state)
    o_ref[...] = _merge_heads([res[3 + 2 * h] for h in range(n_heads)]).astype(o_ref.dtype)


def _sb_attention(qkv, batch, seq, n_pairs, q_col, k_col, v_col):
    T = qkv.shape[0]
    tq = _pick(seq, (256, 128))
    nq = seq // tq
    pairs = _pairs_per_step((4, 2, 1), n_pairs, q_col, k_col, v_col)
    W = pairs * LANES
    qc, kc, vc = q_col // pairs, k_col // pairs, v_col // pairs
    return pl.pallas_call(
        functools.partial(_sb_kernel, pairs=pairs),
        grid=(batch, n_pairs // pairs, nq),
        in_specs=[
            pl.BlockSpec((tq, W), lambda b, p, i: (b * nq + i, qc + p)),
            pl.BlockSpec((seq, W), lambda b, p, i: (b, kc + p)),
            pl.BlockSpec((seq, W), lambda b, p, i: (b, vc + p)),
        ],
        out_specs=pl.BlockSpec((tq, W), lambda b, p, i: (b * nq + i, p)),
        out_shape=jax.ShapeDtypeStruct((T, n_pairs * LANES), BF16),
        compiler_params=_params(("arbitrary", "arbitrary", "arbitrary")),
        name="sb_attention",
    )(qkv, qkv, qkv)


def _fox_kernel(q_ref, k_ref, v_ref, cq_ref, ck_ref, o_ref, *, pairs):
    tq = q_ref.shape[0]
    i = pl.program_id(2)
    n_heads = 2 * pairs
    q_heads = _masked_heads(q_ref, pairs)
    cq_heads = [cq_ref[h // 2][:, (h % 2):(h % 2) + 1] for h in range(n_heads)]
    row = lax.broadcasted_iota(jnp.int32, (tq, tq), 0)
    col = lax.broadcasted_iota(jnp.int32, (tq, tq), 1)

    def block(j, carry, diagonal):
        start = pl.multiple_of(j * tq, tq)
        out = []
        for h in range(n_heads):
            lanes = slice((h // 2) * LANES, (h // 2 + 1) * LANES)
            m, l, acc = carry[3 * h], carry[3 * h + 1], carry[3 * h + 2]
            ck = ck_ref[0, h // 2, j]
            logits = _dot_nt(q_heads[h], k_ref[pl.ds(start, tq), lanes])
            logits = logits + cq_heads[h] - ck[(h % 2):(h % 2) + 1, :]
            if diagonal:
                logits = jnp.where(col <= row, logits, NEG_BIG)
            m_new = jnp.maximum(m, jnp.max(logits, axis=1, keepdims=True))
            alpha = jnp.exp(m - m_new)
            p = jnp.exp(logits - m_new)
            l = l * alpha + jnp.sum(p, axis=1, keepdims=True)
            acc = acc * alpha + _dot(p.astype(BF16), v_ref[pl.ds(start, tq), lanes])
            out += [m_new, l, acc]
        return tuple(out)

    init = (jnp.full((tq, 1), NEG_BIG, F32), jnp.zeros((tq, 1), F32),
            jnp.zeros((tq, LANES), F32)) * n_heads
    state = block(i, init, True)
    res = lax.fori_loop(0, i, lambda j, c: block(j, c, False), state)
    outs = [res[3 * h + 2] / res[3 * h + 1] for h in range(n_heads)]
    o_ref[...] = _merge_heads(outs).astype(o_ref.dtype)


def _fox_attention(qkv, cq, ck, batch, seq, n_pairs, q_col, k_col, v_col):
    T = qkv.shape[0]
    tq = ck.shape[-1]
    nq = seq // tq
    pairs = _pairs_per_step((FOX_PAIRS, 1), n_pairs, q_col, k_col, v_col)
    W = pairs * LANES
    qc, kc, vc = q_col // pairs, k_col // pairs, v_col // pairs
    return pl.pallas_call(
        functools.partial(_fox_kernel, pairs=pairs),
        grid=(batch, n_pairs // pairs, nq),
        in_specs=[
            pl.BlockSpec((tq, W), lambda b, p, i: (b * nq + i, qc + p)),
            pl.BlockSpec((seq, W), lambda b, p, i: (b, kc + p)),
            pl.BlockSpec((seq, W), lambda b, p, i: (b, vc + p)),
            pl.BlockSpec((pairs, tq, 2), lambda b, p, i: (p, b * nq + i, 0)),
            pl.BlockSpec((1, pairs, nq, 2, tq), lambda b, p, i: (b, p, 0, 0, 0)),
        ],
        out_specs=pl.BlockSpec((tq, W), lambda b, p, i: (b * nq + i, p)),
        out_shape=jax.ShapeDtypeStruct((T, n_pairs * LANES), BF16),
        compiler_params=_params(("arbitrary", "arbitrary", "arbitrary")),
        name="fox_attention",
    )(qkv, qkv, qkv, cq, ck)


def _conv_kernel(a_ref, g_ref, w_ref, cb_ref, lg_ref, lb_ref, o_ref, u_ref, *, width, halo):
    ts = a_ref.shape[0]

    @pl.when(pl.program_id(1) == 0)
    def _():
        u_ref[0:halo, :] = jnp.zeros((halo, u_ref.shape[1]), F32)

    u_ref[halo:halo + ts, :] = a_ref[...] * _sigmoid(g_ref[...])
    acc = jnp.zeros(a_ref.shape, F32) + cb_ref[...]
    for k in range(width):
        off = halo - (width - 1) + k
        acc = acc + w_ref[k:k + 1, :] * u_ref[off:off + ts, :]
    y = _ln_rows(acc) * lg_ref[...] + lb_ref[...]
    o_ref[...] = (y * _sigmoid(y)).astype(o_ref.dtype)
    u_ref[0:halo, :] = u_ref[ts:ts + halo, :]


def _conv_module(rest, conv_w, conv_b, ln_g, ln_b, batch, seq):
    T = rest.shape[0]
    width, C = conv_w.shape
    halo = -(-(width - 1) // SUBLANES) * SUBLANES
    ts = _pick(seq, (128,))
    ns = seq // ts
    kern = functools.partial(_conv_kernel, width=width, halo=halo)
    vec = lambda: pl.BlockSpec((1, C), lambda b, s: (0, 0))
    return pl.pallas_call(
        kern,
        grid=(batch, ns),
        in_specs=[
            pl.BlockSpec((ts, C), lambda b, s: (b * ns + s, 0)),
            pl.BlockSpec((ts, C), lambda b, s: (b * ns + s, 1)),
            pl.BlockSpec((width, C), lambda b, s: (0, 0)),
            vec(), vec(), vec(),
        ],
        out_specs=pl.BlockSpec((ts, C), lambda b, s: (b * ns + s, 0)),
        out_shape=jax.ShapeDtypeStruct((T, C), BF16),
        scratch_shapes=[pltpu.VMEM((ts + halo, C), F32)],
        compiler_params=_params(("arbitrary", "arbitrary")),
        name="conformer_conv",
    )(rest, rest, conv_w, conv_b.reshape(1, C), ln_g.reshape(1, C), ln_b.reshape(1, C))


def _outproj_kernel(osb_ref, ofx_ref, ocv_ref, w_ref, x_ref, gt_ref, g_ref, b_ref,
                    sc_ref, sh_ref, rwh_ref, rwl_ref, rb_ref,
                    x1_ref, h2_ref, lg_ref, *, alpha):
    d_sb = osb_ref.shape[1]
    d_fx = ofx_ref.shape[1]
    y = _dot(osb_ref[...], w_ref[0:d_sb, :])
    y = y + _dot(ofx_ref[...], w_ref[d_sb:d_sb + d_fx, :])
    y = y + _dot(ocv_ref[...], w_ref[d_sb + d_fx:, :])
    x1 = _ln_rows(alpha * x_ref[...] + gt_ref[0] * y) * g_ref[...] + b_ref[...]
    x1_ref[...] = x1
    h2 = _ln_rows(x1) * (1.0 + sc_ref[0]) + sh_ref[0]
    h2_ref[...] = h2
    h_hi, h_lo = _split2(h2)
    lg = _dot(h_hi, rwh_ref[...]) + _dot(h_lo, rwh_ref[...]) + _dot(h_hi, rwl_ref[...])
    lg_ref[...] = lg + rb_ref[...]


def _outproj(o_sb, o_fx, o_cv, w_out, x2d, gt, g, b, sc2, sh2, rw_hi, rw_lo, rb, seq, alpha):
    T, D = x2d.shape
    tm = _pick(seq, (256, 128))
    per_b = seq // tm
    rowblk = lambda n: pl.BlockSpec((tm, n), lambda i: (i, 0))
    full = lambda r, c: pl.BlockSpec((r, c), lambda i: (0, 0))
    perb = lambda: pl.BlockSpec((1, 1, D), lambda i: (i // per_b, 0, 0))
    return pl.pallas_call(
        functools.partial(_outproj_kernel, alpha=alpha),
        grid=(T // tm,),
        in_specs=[
            rowblk(o_sb.shape[1]), rowblk(o_fx.shape[1]), rowblk(o_cv.shape[1]),
            full(D, D), rowblk(D), perb(), full(1, D), full(1, D), perb(), perb(),
            full(D, LANES), full(D, LANES), full(1, LANES),
        ],
        out_specs=[rowblk(D), rowblk(D), rowblk(LANES)],
        out_shape=[jax.ShapeDtypeStruct((T, D), F32), jax.ShapeDtypeStruct((T, D), F32),
                   jax.ShapeDtypeStruct((T, LANES), F32)],
        compiler_params=_params(("arbitrary",)),
        name="out_proj_norm_router",
    )(o_sb, o_fx, o_cv, w_out, x2d, gt, g, b, sc2, sh2, rw_hi, rw_lo, rb)


def _route_kernel(lg_ref, e_ref, w_ref, *, n_groups, epg):
    L = lg_ref[...]
    lane_i = lax.broadcasted_iota(jnp.int32, L.shape, 1)
    lane = lane_i.astype(F32)
    far = 1e6
    is_grp = lane < n_groups
    l1 = jnp.where(is_grp, L, NEG_BIG)
    m1 = jnp.max(l1, axis=1, keepdims=True)
    grp = jnp.min(jnp.where(is_grp & (l1 == m1), lane, far), axis=1, keepdims=True)
    s1 = jnp.sum(jnp.where(is_grp, jnp.exp(l1 - m1), 0.0), axis=1, keepdims=True)
    p_grp = 1.0 / s1
    lo = n_groups + grp * epg
    in_grp = (lane >= lo) & (lane < lo + epg)
    l2 = jnp.where(in_grp, L, NEG_BIG)
    m2 = jnp.max(l2, axis=1, keepdims=True)
    e2 = jnp.where(in_grp, jnp.exp(l2 - m2), -1.0)
    v1 = jnp.max(e2, axis=1, keepdims=True)
    i1 = jnp.min(jnp.where(e2 == v1, lane, far), axis=1, keepdims=True)
    e2b = jnp.where(lane == i1, -1.0, e2)
    v2 = jnp.max(e2b, axis=1, keepdims=True)
    i2 = jnp.min(jnp.where(e2b == v2, lane, far), axis=1, keepdims=True)
    den = v1 + v2
    w0 = p_grp * v1 / den
    w1 = p_grp * v2 / den
    ids = jnp.where(lane_i == 0, i1 - n_groups, jnp.where(lane_i == 1, i2 - n_groups, 0.0))
    e_ref[...] = ids.astype(jnp.int32)
    w_ref[...] = jnp.where(lane_i == 0, w0, jnp.where(lane_i == 1, w1, 0.0))


def _route(logits, n_groups, epg):
    T = logits.shape[0]
    tm = _pick(T, (512, 256, 128))
    blk = lambda: pl.BlockSpec((tm, LANES), lambda i: (i, 0))
    return pl.pallas_call(
        functools.partial(_route_kernel, n_groups=n_groups, epg=epg),
        grid=(T // tm,),
        in_specs=[blk()],
        out_specs=[blk(), blk()],
        out_shape=[jax.ShapeDtypeStruct((T, LANES), jnp.int32),
                   jax.ShapeDtypeStruct((T, LANES), F32)],
        compiler_params=_params(("arbitrary",)),
        name="route_topk",
    )(logits)


def _plan_kernel(e_ref, dest_ref, be_ref, cnt_ref, base_ref, *, n_experts):
    p = pl.program_id(0)
    i = pl.program_id(1)
    tm = e_ref.shape[0]
    lane = lax.broadcasted_iota(jnp.int32, (tm, LANES), 1)
    e = e_ref[...]
    hot0 = lane == e[:, 0:1]
    hot1 = lane == e[:, 1:2]
    both = jnp.where(hot0 | hot1, 1.0, 0.0)

    @pl.when((p == 0) & (i == 0))
    def _():
        cnt_ref[...] = jnp.zeros_like(cnt_ref)

    @pl.when(p == 0)
    def _():
        cnt_ref[...] += jnp.sum(both, axis=0, keepdims=True)

    @pl.when((p == 1) & (i == 0))
    def _():
        nblk = jnp.floor((cnt_ref[...] + (MOE_BLOCK - 1)) * (1.0 / MOE_BLOCK))
        r = lax.broadcasted_iota(jnp.int32, (LANES, LANES), 0)
        c = lax.broadcasted_iota(jnp.int32, (LANES, LANES), 1)
        before = jnp.where(r < c, 1.0, 0.0).astype(BF16)
        nb8 = jnp.broadcast_to(nblk, (SUBLANES, LANES)).astype(BF16)
        excl = _dot(nb8, before)[0:1, :]
        base_ref[...] = excl * MOE_BLOCK
        cnt_ref[...] = jnp.zeros_like(cnt_ref)
        incl = excl + nblk
        nb_rows = be_ref.shape[0]
        bidx = lax.broadcasted_iota(jnp.int32, (nb_rows, LANES), 0).astype(F32)
        lane_b = lax.broadcasted_iota(jnp.int32, (nb_rows, LANES), 1)
        done = jnp.where((incl <= bidx) & (lane_b < n_experts), 1.0, 0.0)
        be = jnp.sum(done, axis=1, keepdims=True)
        be_ref[...] = jnp.broadcast_to(be, (nb_rows, LANES)).astype(jnp.int32)

    @pl.when(p == 1)
    def _():
        r = lax.broadcasted_iota(jnp.int32, (tm, tm), 0)
        c = lax.broadcasted_iota(jnp.int32, (tm, tm), 1)
        earlier = jnp.where(c < r, 1.0, 0.0).astype(BF16)
        tot = _dot(earlier, both.astype(BF16)) + cnt_ref[...] + base_ref[...]
        d0 = jnp.sum(jnp.where(hot0, tot, 0.0), axis=1, keepdims=True)
        d1 = jnp.sum(jnp.where(hot1, tot, 0.0), axis=1, keepdims=True)
        dest = jnp.where(lane == 0, d0, jnp.where(lane == 1, d1, 0.0))
        dest_ref[...] = dest.astype(jnp.int32)
        cnt_ref[...] += jnp.sum(both, axis=0, keepdims=True)


def _plan(e_lanes, n_experts, n_blocks):
    T = e_lanes.shape[0]
    tm = _pick(T, (256, 128))
    nb_rows = -(-n_blocks // SUBLANES) * SUBLANES
    return pl.pallas_call(
        functools.partial(_plan_kernel, n_experts=n_experts),
        grid=(2, T // tm),
        in_specs=[pl.BlockSpec((tm, LANES), lambda p, i: (i, 0))],
        out_specs=[pl.BlockSpec((tm, LANES), lambda p, i: (i * p, 0)),
                   pl.BlockSpec((nb_rows, LANES), lambda p, i: (0, 0))],
        out_shape=[jax.ShapeDtypeStruct((T, LANES), jnp.int32),
                   jax.ShapeDtypeStruct((nb_rows, LANES), jnp.int32)],
        scratch_shapes=[pltpu.VMEM((1, LANES), F32), pltpu.VMEM((1, LANES), F32)],
        compiler_params=_params(("arbitrary", "arbitrary")),
        name="dispatch_plan",
    )(e_lanes)


def _row_copy(src, src_row, dst, dst_row, sem):
    return pltpu.make_async_copy(src.at[pl.ds(src_row, 1)], dst.at[pl.ds(dst_row, 1)], sem)


def _slotmap_kernel(dest_ref, tok_ref):
    def clear(s, carry):
        tok_ref[s] = 0
        return carry

    lax.fori_loop(0, tok_ref.shape[0], clear, 0, unroll=8)

    def put(t, carry):
        tok_ref[dest_ref[2 * t]] = t
        tok_ref[dest_ref[2 * t + 1]] = t
        return carry

    lax.fori_loop(0, dest_ref.shape[0] // 2, put, 0, unroll=8)


def _slot_map(dest_flat, n_slots):
    return pl.pallas_call(
        _slotmap_kernel,
        in_specs=[pl.BlockSpec(memory_space=pltpu.SMEM)],
        out_specs=pl.BlockSpec(memory_space=pltpu.SMEM),
        out_shape=jax.ShapeDtypeStruct((n_slots,), jnp.int32),
        name="moe_slot_map",
    )(dest_flat)


def _expert_kernel(be_ref, nu_ref, tok_ref, h_ref, wg_ref, wu_ref, wd_ref, o_ref, xbuf, sems):
    del be_ref
    b = pl.program_id(0)
    n_used = nu_ref[0]
    slot = b % 2

    def gather(blk, s):
        def issue(r, carry):
            _row_copy(h_ref, tok_ref[blk * MOE_BLOCK + r], xbuf.at[s], r, sems.at[s]).start()
            return carry

        lax.fori_loop(0, MOE_BLOCK, issue, 0, unroll=8)

    @pl.when(b == 0)
    def _():
        gather(0, 0)

    @pl.when(b + 1 < n_used)
    def _():
        gather(b + 1, 1 - slot)

    @pl.when(b < n_used)
    def _():
        def drain(r, carry):
            _row_copy(h_ref, 0, xbuf.at[slot], 0, sems.at[slot]).wait()
            return carry

        lax.fori_loop(0, MOE_BLOCK, drain, 0, unroll=8)
        x = xbuf[slot].astype(BF16)
        g = _dot(x, wg_ref[0, 0])
        u = _dot(x, wu_ref[0, 0])
        hid = (g * _sigmoid(g)) * u
        o_ref[...] = _dot(hid.astype(BF16), wd_ref[0, 0])

    @pl.when(b >= n_used)
    def _():
        o_ref[...] = jnp.zeros_like(o_ref)


def _experts(block_expert, n_used, slot_tok, h2, w_gate, w_up, w_down, layer, n_blocks):
    D = h2.shape[1]
    DE = w_gate.shape[3]
    wmap = lambda b, be, nu, tok: (layer, be[jnp.minimum(b, nu[0] - 1)], 0, 0)
    return pl.pallas_call(
        _expert_kernel,
        grid_spec=pltpu.PrefetchScalarGridSpec(
            num_scalar_prefetch=3,
            grid=(n_blocks,),
            in_specs=[
                pl.BlockSpec(memory_space=pl.ANY),
                pl.BlockSpec((1, 1, D, DE), wmap),
                pl.BlockSpec((1, 1, D, DE), wmap),
                pl.BlockSpec((1, 1, DE, D), wmap),
            ],
            out_specs=pl.BlockSpec((MOE_BLOCK, D), lambda b, be, nu, tok: (b, 0)),
            scratch_shapes=[pltpu.VMEM((2, MOE_BLOCK, D), F32), pltpu.SemaphoreType.DMA((2,))],
        ),
        out_shape=jax.ShapeDtypeStruct((n_blocks * MOE_BLOCK, D), F32),
        compiler_params=_params(("arbitrary",)),
        name="moe_experts",
    )(block_expert, n_used, slot_tok, h2, w_gate, w_up, w_down)


def _combine_kernel(dest_ref, yb_ref, w_ref, x_ref, gt_ref, g_ref, b_ref, o_ref,
                    buf_ref, sems, *, alpha):
    tm = x_ref.shape[0]
    i = pl.program_id(0)
    slot = i % 2

    def gather(tile, s):
        def issue(r, carry):
            t = tile * tm + r
            _row_copy(yb_ref, dest_ref[2 * t], buf_ref.at[s, 0], r, sems.at[s]).start()
            _row_copy(yb_ref, dest_ref[2 * t + 1], buf_ref.at[s, 1], r, sems.at[s]).start()
            return carry

        lax.fori_loop(0, tm, issue, 0, unroll=8)

    @pl.when(i == 0)
    def _():
        gather(0, 0)

    @pl.when(i + 1 < pl.num_programs(0))
    def _():
        gather(i + 1, 1 - slot)

    def drain(r, carry):
        _row_copy(yb_ref, 0, buf_ref.at[slot, 0], 0, sems.at[slot]).wait()
        return carry

    lax.fori_loop(0, 2 * tm, drain, 0, unroll=8)
    w = w_ref[...]
    y = w[:, 0:1] * buf_ref[slot, 0] + w[:, 1:2] * buf_ref[slot, 1]
    v = alpha * x_ref[...] + gt_ref[0] * y
    o_ref[...] = _ln_rows(v) * g_ref[...] + b_ref[...]


def _combine(dest_flat, yb, w_lanes, x1, gt, g, b, seq, alpha):
    T, D = x1.shape
    tm = _pick(seq, (256, 128))
    per_b = seq // tm
    return pl.pallas_call(
        functools.partial(_combine_kernel, alpha=alpha),
        grid_spec=pltpu.PrefetchScalarGridSpec(
            num_scalar_prefetch=1,
            grid=(T // tm,),
            in_specs=[
                pl.BlockSpec(memory_space=pl.ANY),
                pl.BlockSpec((tm, LANES), lambda i, d: (i, 0)),
                pl.BlockSpec((tm, D), lambda i, d: (i, 0)),
                pl.BlockSpec((1, 1, D), lambda i, d: (i // per_b, 0, 0)),
                pl.BlockSpec((1, D), lambda i, d: (0, 0)),
                pl.BlockSpec((1, D), lambda i, d: (0, 0)),
            ],
            out_specs=pl.BlockSpec((tm, D), lambda i, d: (i, 0)),
            scratch_shapes=[pltpu.VMEM((2, 2, tm, D), F32), pltpu.SemaphoreType.DMA((2,))],
        ),
        out_shape=jax.ShapeDtypeStruct((T, D), F32),
        compiler_params=_params(("arbitrary",)),
        name="moe_combine_norm",
    )(dest_flat, yb, w_lanes, x1, gt, g, b)


def kernel(x, c, ada_w, ada_b, w_in, b_forget, conv_w, conv_b, conv_ln_g, conv_ln_b, w_out,
           ln1_g, ln1_b, r1_w, r1_b, r2_w, r2_b, w_gate, w_up, w_down, ln2_g, ln2_b):
    B, S, D = x.shape
    L = ada_w.shape[0]
    T = B * S
    alpha = float((2 * L) ** 0.25)
    d_sb, d_fx, c_cv = D // 4, D // 2, D // 4
    n_fx = d_fx // HEAD_DIM
    n_groups = r1_w.shape[-1]
    epg = r2_w.shape[-1]
    n_experts = n_groups * epg
    n_blocks = (2 * T) // MOE_BLOCK + n_experts
    n_slots = n_blocks * MOE_BLOCK
    qkv_cols = 3 * d_sb + 3 * d_fx
    tq = _pick(S, (512, 256, 128))

    rows = -(-B // SUBLANES) * SUBLANES
    c_pad = jnp.zeros((rows, D), F32).at[:B].set(c)
    mod_all = _ada(c_pad, ada_w, ada_b)

    wg16, wu16, wd16 = w_gate.astype(BF16), w_up.astype(BF16), w_down.astype(BF16)
    x2d = x.reshape(T, D)
    for l in range(L):
        mod = mod_all[l, :B]
        sh1, sc1, gt1, sh2, sc2, gt2 = [m.reshape(B, 1, D) for m in jnp.split(mod, 6, axis=-1)]

        h1 = _lnmod(x2d, sc1, sh1, S)
        w_qkv = w_in[l, :, :qkv_cols].astype(BF16)
        f_lo = qkv_cols
        g_lo = qkv_cols + n_fx
        w_rest = jnp.concatenate(
            [w_in[l, :, g_lo:g_lo + 2 * c_cv], w_in[l, :, f_lo:f_lo + n_fx],
             jnp.zeros((D, LANES - n_fx), F32)], axis=1).astype(BF16)
        qkv = _matmul(h1, w_qkv, BF16, "proj_qkv")
        rest = _matmul(h1, w_rest, F32, "proj_glu_forget")

        b_pad = jnp.zeros((1, LANES), F32).at[0, :n_fx].set(b_forget[l])
        cum = _forget_cumsum(rest, b_pad, B, S, (2 * c_cv) // LANES)[:, :n_fx]
        cq = cum.reshape(T, n_fx // 2, 2).transpose(1, 0, 2)
        ck = cum.reshape(B, S // tq, tq, n_fx // 2, 2).transpose(0, 3, 1, 4, 2)

        nb = LANES
        o_sb = _sb_attention(qkv, B, S, d_sb // nb, 0, d_sb // nb, 2 * d_sb // nb)
        fx0 = 3 * d_sb // nb
        o_fx = _fox_attention(qkv, cq, ck, B, S, d_fx // nb, fx0, fx0 + d_fx // nb,
                              fx0 + 2 * d_fx // nb)
        o_cv = _conv_module(rest, conv_w[l], conv_b[l], conv_ln_g[l], conv_ln_b[l], B, S)

        rw = jnp.concatenate(
            [r1_w[l], r2_w[l].transpose(1, 0, 2).reshape(D, n_experts),
             jnp.zeros((D, LANES - n_groups - n_experts), F32)], axis=1)
        rw_hi = rw.astype(BF16)
        rw_lo = (rw - rw_hi.astype(F32)).astype(BF16)
        rb = jnp.zeros((1, LANES), F32).at[0, :n_groups].set(r1_b[l])
        rb = rb.at[0, n_groups:n_groups + n_experts].set(r2_b[l].reshape(-1))
        x1, h2, logits = _outproj(
            o_sb, o_fx, o_cv, w_out[l].astype(BF16), x2d, gt1,
            ln1_g[l].reshape(1, D), ln1_b[l].reshape(1, D), sc2, sh2, rw_hi, rw_lo, rb, S, alpha)

        e_lanes, w_lanes = _route(logits, n_groups, epg)
        dest_lanes, be_lanes = _plan(e_lanes, n_experts, n_blocks)
        dest_flat = dest_lanes[:, :2].reshape(-1)
        be = be_lanes[:n_blocks, 0]
        n_used = jnp.sum((be < n_experts).astype(jnp.int32)).reshape(1)
        block_expert = jnp.minimum(be, n_experts - 1)
        slot_tok = _slot_map(dest_flat, n_slots)
        yb = _experts(block_expert, n_used, slot_tok, h2, wg16, wu16, wd16, l, n_blocks)
        x2d = _combine(dest_flat, yb, w_lanes, x1, gt2,
                       ln2_g[l].reshape(1, D), ln2_b[l].reshape(1, D), S, alpha)
    return x2d.reshape(B, S, D)
```

```python
import functools

import jax
import jax.numpy as jnp
from jax import lax
from jax.experimental import pallas as pl
from jax.experimental.pallas import tpu as pltpu

LN_EPS = 1e-5
HEAD_DIM = 64
LANES = 128
SUBLANES = 8
MOE_BLOCK = 256
NEG_BIG = -1e30
SB_DEAD_LOG = -104.0
FOX_PAIRS = 2
VMEM_LIMIT = 56 * 1024 * 1024

F32 = jnp.float32
BF16 = jnp.bfloat16


def _pick(n, cands):
    for c in cands:
        if n % c == 0:
            return c
    return n


def _params(sem):
    return pltpu.CompilerParams(dimension_semantics=sem, vmem_limit_bytes=VMEM_LIMIT)


def _ln_rows(v):
    mu = jnp.mean(v, axis=-1, keepdims=True)
    d = v - mu
    var = jnp.mean(d * d, axis=-1, keepdims=True)
    return d * lax.rsqrt(var + LN_EPS)


def _log_sigmoid(z):
    return jnp.minimum(z, 0.0) - jnp.log1p(jnp.exp(-jnp.abs(z)))


def _sigmoid(z):
    return 1.0 / (1.0 + jnp.exp(-z))


def _split2(v):
    hi = v.astype(BF16)
    lo = (v - hi.astype(F32)).astype(BF16)
    return hi, lo


def _dot(a, b):
    return jnp.dot(a, b, preferred_element_type=F32)


def _dot_nt(a, b):
    return lax.dot_general(a, b, (((1,), (1,)), ((), ())), preferred_element_type=F32)


def _ada_kernel(c_ref, w_ref, b_ref, o_ref):
    c = c_ref[...]
    s = c * _sigmoid(c)
    s_hi, s_lo = _split2(s)
    w_hi, w_lo = _split2(w_ref[0])
    acc = _dot(s_hi, w_hi) + _dot(s_lo, w_hi) + _dot(s_hi, w_lo)
    o_ref[0] = acc + b_ref[0]


def _ada(c_pad, ada_w, ada_b):
    L, D, N = ada_w.shape
    rows = c_pad.shape[0]
    tn = _pick(N, (512, 256, 128))
    return pl.pallas_call(
        _ada_kernel,
        grid=(L, N // tn),
        in_specs=[
            pl.BlockSpec((rows, D), lambda l, n: (0, 0)),
            pl.BlockSpec((1, D, tn), lambda l, n: (l, 0, n)),
            pl.BlockSpec((1, 1, tn), lambda l, n: (l, 0, n)),
        ],
        out_specs=pl.BlockSpec((1, rows, tn), lambda l, n: (l, 0, n)),
        out_shape=jax.ShapeDtypeStruct((L, rows, N), F32),
        compiler_params=_params(("arbitrary", "arbitrary")),
        name="ada_mod",
    )(c_pad, ada_w, ada_b.reshape(L, 1, N))


def _lnmod_kernel(x_ref, sc_ref, sh_ref, o_ref):
    h = _ln_rows(x_ref[...]) * (1.0 + sc_ref[0]) + sh_ref[0]
    o_ref[...] = h.astype(o_ref.dtype)


def _lnmod(x2d, sc, sh, seq):
    T, D = x2d.shape
    tm = _pick(seq, (512, 256, 128))
    per_b = seq // tm
    return pl.pallas_call(
        _lnmod_kernel,
        grid=(T // tm,),
        in_specs=[
            pl.BlockSpec((tm, D), lambda i: (i, 0)),
            pl.BlockSpec((1, 1, D), lambda i: (i // per_b, 0, 0)),
            pl.BlockSpec((1, 1, D), lambda i: (i // per_b, 0, 0)),
        ],
        out_specs=pl.BlockSpec((tm, D), lambda i: (i, 0)),
        out_shape=jax.ShapeDtypeStruct((T, D), BF16),
        compiler_params=_params(("arbitrary",)),
        name="ln_mod",
    )(x2d, sc, sh)


def _mm_kernel(a_ref, b_ref, o_ref):
    o_ref[...] = _dot(a_ref[...], b_ref[...]).astype(o_ref.dtype)


def _matmul(a, w, out_dtype, name):
    M, K = a.shape
    N = w.shape[1]
    tm = _pick(M, (1024, 512, 256, 128))
    tn = _pick(N, (512, 384, 256, 128))
    return pl.pallas_call(
        _mm_kernel,
        grid=(M // tm, N // tn),
        in_specs=[
            pl.BlockSpec((tm, K), lambda i, j: (i, 0)),
            pl.BlockSpec((K, tn), lambda i, j: (0, j)),
        ],
        out_specs=pl.BlockSpec((tm, tn), lambda i, j: (i, j)),
        out_shape=jax.ShapeDtypeStruct((M, N), out_dtype),
        compiler_params=_params(("arbitrary", "arbitrary")),
        name=name,
    )(a, w)


def _cum_kernel(f_ref, b_ref, o_ref, carry_ref):
    @pl.when(pl.program_id(1) == 0)
    def _():
        carry_ref[...] = jnp.zeros_like(carry_ref)

    ts = f_ref.shape[0]
    lf = _log_sigmoid(f_ref[...] + b_ref[...])
    p1 = lf.astype(BF16)
    r1 = lf - p1.astype(F32)
    p2 = r1.astype(BF16)
    p3 = (r1 - p2.astype(F32)).astype(BF16)
    row = lax.broadcasted_iota(jnp.int32, (ts, ts), 0)
    col = lax.broadcasted_iota(jnp.int32, (ts, ts), 1)
    tri = jnp.where(col <= row, 1.0, 0.0).astype(BF16)
    cum = _dot(tri, p1) + _dot(tri, p2) + _dot(tri, p3) + carry_ref[...]
    o_ref[...] = cum
    carry_ref[...] = cum[ts - 1:ts, :]


def _forget_cumsum(rest, b_pad, batch, seq, col_block):
    T = rest.shape[0]
    ts = _pick(seq, (256, 128))
    ns = seq // ts
    return pl.pallas_call(
        _cum_kernel,
        grid=(batch, ns),
        in_specs=[
            pl.BlockSpec((ts, LANES), lambda b, s: (b * ns + s, col_block)),
            pl.BlockSpec((1, LANES), lambda b, s: (0, 0)),
        ],
        out_specs=pl.BlockSpec((ts, LANES), lambda b, s: (b * ns + s, 0)),
        out_shape=jax.ShapeDtypeStruct((T, LANES), F32),
        scratch_shapes=[pltpu.VMEM((1, LANES), F32)],
        compiler_params=_params(("arbitrary", "arbitrary")),
        name="forget_cumsum",
    )(rest, b_pad)


def _masked_heads(q_ref, pairs):
    lane = lax.broadcasted_iota(jnp.int32, (1, LANES), 1)
    heads = []
    for p in range(pairs):
        q2 = q_ref[:, p * LANES:(p + 1) * LANES] * jnp.asarray(HEAD_DIM ** -0.5, q_ref.dtype)
        zero = jnp.zeros_like(q2)
        heads += [jnp.where(lane < HEAD_DIM, q2, zero), jnp.where(lane >= HEAD_DIM, q2, zero)]
    return heads


def _pairs_per_step(cands, *counts):
    return next(c for c in cands if all(n % c == 0 for n in counts))


def _store_value_transposed(v_ref, vt_ref, first):
    pairs, nk, _, two_tk = vt_ref.shape
    tk = two_tk // 2
    for p in range(pairs):
        for j in range(nk):
            vt = v_ref[j * tk:(j + 1) * tk, p * LANES:(p + 1) * LANES].astype(F32).T
            both = jnp.concatenate([jnp.where(first, vt, 0.0), jnp.where(first, 0.0, vt)], axis=1)
            vt_ref[p, j] = both.astype(BF16)


def _sb_kernel(q_ref, k_ref, v_ref, o_ref, vt_ref, *, pairs):
    tq = q_ref.shape[0]
    i = pl.program_id(2)
    q_heads = _masked_heads(q_ref, pairs)
    first = lax.broadcasted_iota(jnp.int32, (LANES, 1), 0) < HEAD_DIM

    @pl.when(i == 0)
    def _():
        _store_value_transposed(v_ref, vt_ref, first)

    key = lax.broadcasted_iota(jnp.int32, (tq, tq), 0)
    qry = lax.broadcasted_iota(jnp.int32, (tq, tq), 1)
    later_keys = jnp.where(qry > key, 1.0, 0.0).astype(BF16)

    def block(j, state, diagonal):
        start = pl.multiple_of(j * tq, tq)
        out = []
        for p in range(pairs):
            kblk = k_ref[pl.ds(start, tq), p * LANES:(p + 1) * LANES]
            weights, runs = [], []
            for s in range(2):
                run = state[3 * p + s]
                z = _dot_nt(kblk, q_heads[2 * p + s])
                ls = _log_sigmoid(z)
                lk = ls - z
                if diagonal:
                    lk = jnp.where(key < qry, lk, 0.0)
                lk_hi, lk_lo = _split2(lk)
                later = _dot(later_keys, lk_hi) + _dot(later_keys, lk_lo)
                w = jnp.exp(ls + later + run)
                if diagonal:
                    w = jnp.where(key < qry, w, 0.0)
                weights.append(w.astype(BF16))
                runs.append(run + jnp.sum(lk, axis=0, keepdims=True))
            acc = state[3 * p + 2] + _dot(vt_ref[p, j], jnp.concatenate(weights, axis=0))
            out += [runs[0], runs[1], acc]
        return tuple(out)

    def live(state):
        top = state[0]
        for p in range(pairs):
            top = jnp.maximum(top, jnp.maximum(state[3 * p], state[3 * p + 1]))
        return (jnp.max(top) > SB_DEAD_LOG).astype(jnp.int32)

    init = (jnp.zeros((1, tq), F32), jnp.zeros((1, tq), F32), jnp.zeros((LANES, tq), F32)) * pairs
    state = block(i, init, True)

    def cond(carry):
        return (carry[0] >= 0) & (carry[1] > 0)

    def body(carry):
        state = block(carry[0], carry[2:], False)
        return (carry[0] - 1, live(state)) + state

    res = lax.while_loop(cond, body, (i - 1, live(state)) + state)
    slabs = [res[2 + 3 * p + 2].T for p in range(pairs)]
    o = slabs[0] if pairs == 1 else jnp.concatenate(slabs, axis=1)
    o_ref[...] = o.astype(o_ref.dtype)


def _sb_attention(qkv, batch, seq, n_pairs, q_col, k_col, v_col):
    T = qkv.shape[0]
    tq = _pick(seq, (256, 128))
    nq = seq // tq
    pairs = _pairs_per_step((4, 2, 1), n_pairs, q_col, k_col, v_col)
    W = pairs * LANES
    qc, kc, vc = q_col // pairs, k_col // pairs, v_col // pairs
    return pl.pallas_call(
        functools.partial(_sb_kernel, pairs=pairs),
        grid=(batch, n_pairs // pairs, nq),
        in_specs=[
            pl.BlockSpec((tq, W), lambda b, p, i: (b * nq + i, qc + p)),
            pl.BlockSpec((seq, W), lambda b, p, i: (b, kc + p)),
            pl.BlockSpec((seq, W), lambda b, p, i: (b, vc + p)),
        ],
        out_specs=pl.BlockSpec((tq, W), lambda b, p, i: (b * nq + i, p)),
        out_shape=jax.ShapeDtypeStruct((T, n_pairs * LANES), BF16),
        scratch_shapes=[pltpu.VMEM((pairs, nq, LANES, 2 * tq), BF16)],
        compiler_params=_params(("arbitrary", "arbitrary", "arbitrary")),
        name="sb_attention",
    )(qkv, qkv, qkv)


def _fox_kernel(q_ref, k_ref, v_ref, cc_ref, o_ref, vt_ref, *, pairs):
    tq = q_ref.shape[0]
    i = pl.program_id(2)
    q_heads = _masked_heads(q_ref, pairs)
    first = lax.broadcasted_iota(jnp.int32, (LANES, 1), 0) < HEAD_DIM

    @pl.when(i == 0)
    def _():
        _store_value_transposed(v_ref, vt_ref, first)

    key = lax.broadcasted_iota(jnp.int32, (tq, tq), 0)
    qry = lax.broadcasted_iota(jnp.int32, (tq, tq), 1)

    def block(j, carry, diagonal):
        start = pl.multiple_of(j * tq, tq)
        out = []
        for p in range(pairs):
            kblk = k_ref[pl.ds(start, tq), p * LANES:(p + 1) * LANES]
            acc = carry[5 * p + 4]
            stats = []
            for s in range(2):
                m, l = carry[5 * p + 2 * s], carry[5 * p + 2 * s + 1]
                logits = _dot_nt(kblk, q_heads[2 * p + s]) - cc_ref[p, pl.ds(start, tq), s:s + 1]
                if diagonal:
                    logits = jnp.where(key <= qry, logits, NEG_BIG)
                m_new = jnp.maximum(m, jnp.max(logits, axis=0, keepdims=True))
                alpha = jnp.exp(m - m_new)
                prob = jnp.exp(logits - m_new)
                l = l * alpha + jnp.sum(prob, axis=0, keepdims=True)
                stats.append((m_new, l, alpha, prob.astype(BF16)))
            pv = _dot(vt_ref[p, j], jnp.concatenate([stats[0][3], stats[1][3]], axis=0))
            acc = acc * jnp.where(first, stats[0][2], stats[1][2]) + pv
            out += [stats[0][0], stats[0][1], stats[1][0], stats[1][1], acc]
        return tuple(out)

    row_stat = (jnp.full((1, tq), NEG_BIG, F32), jnp.zeros((1, tq), F32))
    init = (row_stat * 2 + (jnp.zeros((LANES, tq), F32),)) * pairs
    state = block(i, init, True)
    res = lax.fori_loop(0, i, lambda j, c: block(j, c, False), state)
    slabs = []
    for p in range(pairs):
        inv = jnp.where(first, 1.0 / res[5 * p + 1], 1.0 / res[5 * p + 3])
        slabs.append((res[5 * p + 4] * inv).T)
    o = slabs[0] if pairs == 1 else jnp.concatenate(slabs, axis=1)
    o_ref[...] = o.astype(o_ref.dtype)


def _fox_attention(qkv, cum_cols, batch, seq, tq, n_pairs, q_col, k_col, v_col):
    T = qkv.shape[0]
    nq = seq // tq
    pairs = _pairs_per_step((FOX_PAIRS, 1), n_pairs, q_col, k_col, v_col)
    W = pairs * LANES
    qc, kc, vc = q_col // pairs, k_col // pairs, v_col // pairs
    return pl.pallas_call(
        functools.partial(_fox_kernel, pairs=pairs),
        grid=(batch, n_pairs // pairs, nq),
        in_specs=[
            pl.BlockSpec((tq, W), lambda b, p, i: (b * nq + i, qc + p)),
            pl.BlockSpec((seq, W), lambda b, p, i: (b, kc + p)),
            pl.BlockSpec((seq, W), lambda b, p, i: (b, vc + p)),
            pl.BlockSpec((pairs, seq, 2), lambda b, p, i: (p, b, 0)),
        ],
        out_specs=pl.BlockSpec((tq, W), lambda b, p, i: (b * nq + i, p)),
        out_shape=jax.ShapeDtypeStruct((T, n_pairs * LANES), BF16),
        scratch_shapes=[pltpu.VMEM((pairs, nq, LANES, 2 * tq), BF16)],
        compiler_params=_params(("arbitrary", "arbitrary", "arbitrary")),
        name="fox_attention",
    )(qkv, qkv, qkv, cum_cols)


def _conv_kernel(a_ref, g_ref, w_ref, cb_ref, lg_ref, lb_ref, o_ref, u_ref, *, width, halo):
    ts = a_ref.shape[0]

    @pl.when(pl.program_id(1) == 0)
    def _():
        u_ref[0:halo, :] = jnp.zeros((halo, u_ref.shape[1]), F32)

    u_ref[halo:halo + ts, :] = a_ref[...] * _sigmoid(g_ref[...])
    acc = jnp.zeros(a_ref.shape, F32) + cb_ref[...]
    for k in range(width):
        off = halo - (width - 1) + k
        acc = acc + w_ref[k:k + 1, :] * u_ref[off:off + ts, :]
    y = _ln_rows(acc) * lg_ref[...] + lb_ref[...]
    o_ref[...] = (y * _sigmoid(y)).astype(o_ref.dtype)
    u_ref[0:halo, :] = u_ref[ts:ts + halo, :]


def _conv_module(rest, conv_w, conv_b, ln_g, ln_b, batch, seq):
    T = rest.shape[0]
    width, C = conv_w.shape
    halo = -(-(width - 1) // SUBLANES) * SUBLANES
    ts = _pick(seq, (128,))
    ns = seq // ts
    kern = functools.partial(_conv_kernel, width=width, halo=halo)
    vec = lambda: pl.BlockSpec((1, C), lambda b, s: (0, 0))
    return pl.pallas_call(
        kern,
        grid=(batch, ns),
        in_specs=[
            pl.BlockSpec((ts, C), lambda b, s: (b * ns + s, 0)),
            pl.BlockSpec((ts, C), lambda b, s: (b * ns + s, 1)),
            pl.BlockSpec((width, C), lambda b, s: (0, 0)),
            vec(), vec(), vec(),
        ],
        out_specs=pl.BlockSpec((ts, C), lambda b, s: (b * ns + s, 0)),
        out_shape=jax.ShapeDtypeStruct((T, C), BF16),
        scratch_shapes=[pltpu.VMEM((ts + halo, C), F32)],
        compiler_params=_params(("arbitrary", "arbitrary")),
        name="conformer_conv",
    )(rest, rest, conv_w, conv_b.reshape(1, C), ln_g.reshape(1, C), ln_b.reshape(1, C))


def _outproj_kernel(osb_ref, ofx_ref, ocv_ref, w_ref, x_ref, gt_ref, g_ref, b_ref,
                    sc_ref, sh_ref, rwh_ref, rwl_ref, rb_ref,
                    x1_ref, h2_ref, lg_ref, *, alpha):
    d_sb = osb_ref.shape[1]
    d_fx = ofx_ref.shape[1]
    y = _dot(osb_ref[...], w_ref[0:d_sb, :])
    y = y + _dot(ofx_ref[...], w_ref[d_sb:d_sb + d_fx, :])
    y = y + _dot(ocv_ref[...], w_ref[d_sb + d_fx:, :])
    x1 = _ln_rows(alpha * x_ref[...] + gt_ref[0] * y) * g_ref[...] + b_ref[...]
    x1_ref[...] = x1
    h2 = _ln_rows(x1) * (1.0 + sc_ref[0]) + sh_ref[0]
    h2_ref[...] = h2
    h_hi, h_lo = _split2(h2)
    lg = _dot(h_hi, rwh_ref[...]) + _dot(h_lo, rwh_ref[...]) + _dot(h_hi, rwl_ref[...])
    lg_ref[...] = lg + rb_ref[...]


def _outproj(o_sb, o_fx, o_cv, w_out, x2d, gt, g, b, sc2, sh2, rw_hi, rw_lo, rb, seq, alpha):
    T, D = x2d.shape
    tm = _pick(seq, (256, 128))
    per_b = seq // tm
    rowblk = lambda n: pl.BlockSpec((tm, n), lambda i: (i, 0))
    full = lambda r, c: pl.BlockSpec((r, c), lambda i: (0, 0))
    perb = lambda: pl.BlockSpec((1, 1, D), lambda i: (i // per_b, 0, 0))
    return pl.pallas_call(
        functools.partial(_outproj_kernel, alpha=alpha),
        grid=(T // tm,),
        in_specs=[
            rowblk(o_sb.shape[1]), rowblk(o_fx.shape[1]), rowblk(o_cv.shape[1]),
            full(D, D), rowblk(D), perb(), full(1, D), full(1, D), perb(), perb(),
            full(D, LANES), full(D, LANES), full(1, LANES),
        ],
        out_specs=[rowblk(D), rowblk(D), rowblk(LANES)],
        out_shape=[jax.ShapeDtypeStruct((T, D), F32), jax.ShapeDtypeStruct((T, D), F32),
                   jax.ShapeDtypeStruct((T, LANES), F32)],
        compiler_params=_params(("arbitrary",)),
        name="out_proj_norm_router",
    )(o_sb, o_fx, o_cv, w_out, x2d, gt, g, b, sc2, sh2, rw_hi, rw_lo, rb)


def _route_kernel(lg_ref, e_ref, w_ref, *, n_groups, epg):
    L = lg_ref[...]
    lane_i = lax.broadcasted_iota(jnp.int32, L.shape, 1)
    lane = lane_i.astype(F32)
    far = 1e6
    is_grp = lane < n_groups
    l1 = jnp.where(is_grp, L, NEG_BIG)
    m1 = jnp.max(l1, axis=1, keepdims=True)
    grp = jnp.min(jnp.where(is_grp & (l1 == m1), lane, far), axis=1, keepdims=True)
    s1 = jnp.sum(jnp.where(is_grp, jnp.exp(l1 - m1), 0.0), axis=1, keepdims=True)
    p_grp = 1.0 / s1
    lo = n_groups + grp * epg
    in_grp = (lane >= lo) & (lane < lo + epg)
    l2 = jnp.where(in_grp, L, NEG_BIG)
    m2 = jnp.max(l2, axis=1, keepdims=True)
    e2 = jnp.where(in_grp, jnp.exp(l2 - m2), -1.0)
    v1 = jnp.max(e2, axis=1, keepdims=True)
    i1 = jnp.min(jnp.where(e2 == v1, lane, far), axis=1, keepdims=True)
    e2b = jnp.where(lane == i1, -1.0, e2)
    v2 = jnp.max(e2b, axis=1, keepdims=True)
    i2 = jnp.min(jnp.where(e2b == v2, lane, far), axis=1, keepdims=True)
    den = v1 + v2
    w0 = p_grp * v1 / den
    w1 = p_grp * v2 / den
    ids = jnp.where(lane_i == 0, i1 - n_groups, jnp.where(lane_i == 1, i2 - n_groups, 0.0))
    e_ref[...] = ids.astype(jnp.int32)
    w_ref[...] = jnp.where(lane_i == 0, w0, jnp.where(lane_i == 1, w1, 0.0))


def _route(logits, n_groups, epg):
    T = logits.shape[0]
    tm = _pick(T, (512, 256, 128))
    blk = lambda: pl.BlockSpec((tm, LANES), lambda i: (i, 0))
    return pl.pallas_call(
        functools.partial(_route_kernel, n_groups=n_groups, epg=epg),
        grid=(T // tm,),
        in_specs=[blk()],
        out_specs=[blk(), blk()],
        out_shape=[jax.ShapeDtypeStruct((T, LANES), jnp.int32),
                   jax.ShapeDtypeStruct((T, LANES), F32)],
        compiler_params=_params(("arbitrary",)),
        name="route_topk",
    )(logits)


def _plan_kernel(e_ref, dest_ref, be_ref, cnt_ref, base_ref, *, n_experts):
    p = pl.program_id(0)
    i = pl.program_id(1)
    tm = e_ref.shape[0]
    lane = lax.broadcasted_iota(jnp.int32, (tm, LANES), 1)
    e = e_ref[...]
    hot0 = lane == e[:, 0:1]
    hot1 = lane == e[:, 1:2]
    both = jnp.where(hot0 | hot1, 1.0, 0.0)

    @pl.when((p == 0) & (i == 0))
    def _():
        cnt_ref[...] = jnp.zeros_like(cnt_ref)

    @pl.when(p == 0)
    def _():
        cnt_ref[...] += jnp.sum(both, axis=0, keepdims=True)

    @pl.when((p == 1) & (i == 0))
    def _():
        nblk = jnp.floor((cnt_ref[...] + (MOE_BLOCK - 1)) * (1.0 / MOE_BLOCK))
        r = lax.broadcasted_iota(jnp.int32, (LANES, LANES), 0)
        c = lax.broadcasted_iota(jnp.int32, (LANES, LANES), 1)
        before = jnp.where(r < c, 1.0, 0.0).astype(BF16)
        nb8 = jnp.broadcast_to(nblk, (SUBLANES, LANES)).astype(BF16)
        excl = _dot(nb8, before)[0:1, :]
        base_ref[...] = excl * MOE_BLOCK
        cnt_ref[...] = jnp.zeros_like(cnt_ref)
        incl = excl + nblk
        nb_rows = be_ref.shape[0]
        bidx = lax.broadcasted_iota(jnp.int32, (nb_rows, LANES), 0).astype(F32)
        lane_b = lax.broadcasted_iota(jnp.int32, (nb_rows, LANES), 1)
        done = jnp.where((incl <= bidx) & (lane_b < n_experts), 1.0, 0.0)
        be = jnp.sum(done, axis=1, keepdims=True)
        be_ref[...] = jnp.broadcast_to(be, (nb_rows, LANES)).astype(jnp.int32)

    @pl.when(p == 1)
    def _():
        r = lax.broadcasted_iota(jnp.int32, (tm, tm), 0)
        c = lax.broadcasted_iota(jnp.int32, (tm, tm), 1)
        earlier = jnp.where(c < r, 1.0, 0.0).astype(BF16)
        tot = _dot(earlier, both.astype(BF16)) + cnt_ref[...] + base_ref[...]
        d0 = jnp.sum(jnp.where(hot0, tot, 0.0), axis=1, keepdims=True)
        d1 = jnp.sum(jnp.where(hot1, tot, 0.0), axis=1, keepdims=True)
        dest = jnp.where(lane == 0, d0, jnp.where(lane == 1, d1, 0.0))
        dest_ref[...] = dest.astype(jnp.int32)
        cnt_ref[...] += jnp.sum(both, axis=0, keepdims=True)


def _plan(e_lanes, n_experts, n_blocks):
    T = e_lanes.shape[0]
    tm = _pick(T, (256, 128))
    nb_rows = -(-n_blocks // SUBLANES) * SUBLANES
    return pl.pallas_call(
        functools.partial(_plan_kernel, n_experts=n_experts),
        grid=(2, T // tm),
        in_specs=[pl.BlockSpec((tm, LANES), lambda p, i: (i, 0))],
        out_specs=[pl.BlockSpec((tm, LANES), lambda p, i: (i * p, 0)),
                   pl.BlockSpec((nb_rows, LANES), lambda p, i: (0, 0))],
        out_shape=[jax.ShapeDtypeStruct((T, LANES), jnp.int32),
                   jax.ShapeDtypeStruct((nb_rows, LANES), jnp.int32)],
        scratch_shapes=[pltpu.VMEM((1, LANES), F32), pltpu.VMEM((1, LANES), F32)],
        compiler_params=_params(("arbitrary", "arbitrary")),
        name="dispatch_plan",
    )(e_lanes)


def _row_copy(src, src_row, dst, dst_row, sem):
    return pltpu.make_async_copy(src.at[pl.ds(src_row, 1)], dst.at[pl.ds(dst_row, 1)], sem)


def _slotmap_kernel(dest_ref, tok_ref):
    def clear(s, carry):
        tok_ref[s] = 0
        return carry

    lax.fori_loop(0, tok_ref.shape[0], clear, 0, unroll=8)

    def put(t, carry):
        tok_ref[dest_ref[2 * t]] = t
        tok_ref[dest_ref[2 * t + 1]] = t
        return carry

    lax.fori_loop(0, dest_ref.shape[0] // 2, put, 0, unroll=8)


def _slot_map(dest_flat, n_slots):
    return pl.pallas_call(
        _slotmap_kernel,
        in_specs=[pl.BlockSpec(memory_space=pltpu.SMEM)],
        out_specs=pl.BlockSpec(memory_space=pltpu.SMEM),
        out_shape=jax.ShapeDtypeStruct((n_slots,), jnp.int32),
        name="moe_slot_map",
    )(dest_flat)


def _expert_kernel(be_ref, nu_ref, tok_ref, h_ref, wg_ref, wu_ref, wd_ref, o_ref, xbuf, sems):
    del be_ref
    b = pl.program_id(0)
    n_used = nu_ref[0]
    slot = b % 2

    def gather(blk, s):
        def issue(r, carry):
            _row_copy(h_ref, tok_ref[blk * MOE_BLOCK + r], xbuf.at[s], r, sems.at[s]).start()
            return carry

        lax.fori_loop(0, MOE_BLOCK, issue, 0, unroll=8)

    @pl.when(b == 0)
    def _():
        gather(0, 0)

    @pl.when(b + 1 < n_used)
    def _():
        gather(b + 1, 1 - slot)

    @pl.when(b < n_used)
    def _():
        def drain(r, carry):
            _row_copy(h_ref, 0, xbuf.at[slot], 0, sems.at[slot]).wait()
            return carry

        lax.fori_loop(0, MOE_BLOCK, drain, 0, unroll=8)
        x = xbuf[slot].astype(BF16)
        g = _dot(x, wg_ref[0, 0])
        u = _dot(x, wu_ref[0, 0])
        hid = (g * _sigmoid(g)) * u
        o_ref[...] = _dot(hid.astype(BF16), wd_ref[0, 0])

    @pl.when(b >= n_used)
    def _():
        o_ref[...] = jnp.zeros_like(o_ref)


def _experts(block_expert, n_used, slot_tok, h2, w_gate, w_up, w_down, layer, n_blocks):
    D = h2.shape[1]
    DE = w_gate.shape[3]
    wmap = lambda b, be, nu, tok: (layer, be[jnp.minimum(b, nu[0] - 1)], 0, 0)
    return pl.pallas_call(
        _expert_kernel,
        grid_spec=pltpu.PrefetchScalarGridSpec(
            num_scalar_prefetch=3,
            grid=(n_blocks,),
            in_specs=[
                pl.BlockSpec(memory_space=pl.ANY),
                pl.BlockSpec((1, 1, D, DE), wmap),
                pl.BlockSpec((1, 1, D, DE), wmap),
                pl.BlockSpec((1, 1, DE, D), wmap),
            ],
            out_specs=pl.BlockSpec((MOE_BLOCK, D), lambda b, be, nu, tok: (b, 0)),
            scratch_shapes=[pltpu.VMEM((2, MOE_BLOCK, D), F32), pltpu.SemaphoreType.DMA((2,))],
        ),
        out_shape=jax.ShapeDtypeStruct((n_blocks * MOE_BLOCK, D), F32),
        compiler_params=_params(("arbitrary",)),
        name="moe_experts",
    )(block_expert, n_used, slot_tok, h2, w_gate, w_up, w_down)


def _combine_kernel(dest_ref, yb_ref, w_ref, x_ref, gt_ref, g_ref, b_ref, o_ref,
                    buf_ref, sems, *, alpha):
    tm = x_ref.shape[0]
    i = pl.program_id(0)
    slot = i % 2

    def gather(tile, s):
        def issue(r, carry):
            t = tile * tm + r
            _row_copy(yb_ref, dest_ref[2 * t], buf_ref.at[s, 0], r, sems.at[s]).start()
            _row_copy(yb_ref, dest_ref[2 * t + 1], buf_ref.at[s, 1], r, sems.at[s]).start()
            return carry

        lax.fori_loop(0, tm, issue, 0, unroll=8)

    @pl.when(i == 0)
    def _():
        gather(0, 0)

    @pl.when(i + 1 < pl.num_programs(0))
    def _():
        gather(i + 1, 1 - slot)

    def drain(r, carry):
        _row_copy(yb_ref, 0, buf_ref.at[slot, 0], 0, sems.at[slot]).wait()
        return carry

    lax.fori_loop(0, 2 * tm, drain, 0, unroll=8)
    w = w_ref[...]
    y = w[:, 0:1] * buf_ref[slot, 0] + w[:, 1:2] * buf_ref[slot, 1]
    v = alpha * x_ref[...] + gt_ref[0] * y
    o_ref[...] = _ln_rows(v) * g_ref[...] + b_ref[...]


def _combine(dest_flat, yb, w_lanes, x1, gt, g, b, seq, alpha):
    T, D = x1.shape
    tm = _pick(seq, (256, 128))
    per_b = seq // tm
    return pl.pallas_call(
        functools.partial(_combine_kernel, alpha=alpha),
        grid_spec=pltpu.PrefetchScalarGridSpec(
            num_scalar_prefetch=1,
            grid=(T // tm,),
            in_specs=[
                pl.BlockSpec(memory_space=pl.ANY),
                pl.BlockSpec((tm, LANES), lambda i, d: (i, 0)),
                pl.BlockSpec((tm, D), lambda i, d: (i, 0)),
                pl.BlockSpec((1, 1, D), lambda i, d: (i // per_b, 0, 0)),
                pl.BlockSpec((1, D), lambda i, d: (0, 0)),
                pl.BlockSpec((1, D), lambda i, d: (0, 0)),
            ],
            out_specs=pl.BlockSpec((tm, D), lambda i, d: (i, 0)),
            scratch_shapes=[pltpu.VMEM((2, 2, tm, D), F32), pltpu.SemaphoreType.DMA((2,))],
        ),
        out_shape=jax.ShapeDtypeStruct((T, D), F32),
        compiler_params=_params(("arbitrary",)),
        name="moe_combine_norm",
    )(dest_flat, yb, w_lanes, x1, gt, g, b)


def kernel(x, c, ada_w, ada_b, w_in, b_forget, conv_w, conv_b, conv_ln_g, conv_ln_b, w_out,
           ln1_g, ln1_b, r1_w, r1_b, r2_w, r2_b, w_gate, w_up, w_down, ln2_g, ln2_b):
    B, S, D = x.shape
    L = ada_w.shape[0]
    T = B * S
    alpha = float((2 * L) ** 0.25)
    d_sb, d_fx, c_cv = D // 4, D // 2, D // 4
    n_fx = d_fx // HEAD_DIM
    n_groups = r1_w.shape[-1]
    epg = r2_w.shape[-1]
    n_experts = n_groups * epg
    n_blocks = (2 * T) // MOE_BLOCK + n_experts
    n_slots = n_blocks * MOE_BLOCK
    qkv_cols = 3 * d_sb + 3 * d_fx
    tq = _pick(S, (512, 256, 128))

    rows = -(-B // SUBLANES) * SUBLANES
    c_pad = jnp.zeros((rows, D), F32).at[:B].set(c)
    mod_all = _ada(c_pad, ada_w, ada_b)

    wg16, wu16, wd16 = w_gate.astype(BF16), w_up.astype(BF16), w_down.astype(BF16)
    x2d = x.reshape(T, D)
    for l in range(L):
        mod = mod_all[l, :B]
        sh1, sc1, gt1, sh2, sc2, gt2 = [m.reshape(B, 1, D) for m in jnp.split(mod, 6, axis=-1)]

        h1 = _lnmod(x2d, sc1, sh1, S)
        w_qkv = w_in[l, :, :qkv_cols].astype(BF16)
        f_lo = qkv_cols
        g_lo = qkv_cols + n_fx
        w_rest = jnp.concatenate(
            [w_in[l, :, g_lo:g_lo + 2 * c_cv], w_in[l, :, f_lo:f_lo + n_fx],
             jnp.zeros((D, LANES - n_fx), F32)], axis=1).astype(BF16)
        qkv = _matmul(h1, w_qkv, BF16, "proj_qkv")
        rest = _matmul(h1, w_rest, F32, "proj_glu_forget")

        b_pad = jnp.zeros((1, LANES), F32).at[0, :n_fx].set(b_forget[l])
        cum = _forget_cumsum(rest, b_pad, B, S, (2 * c_cv) // LANES)[:, :n_fx]
        cum_cols = cum.reshape(T, n_fx // 2, 2).transpose(1, 0, 2)

        nb = LANES
        o_sb = _sb_attention(qkv, B, S, d_sb // nb, 0, d_sb // nb, 2 * d_sb // nb)
        fx0 = 3 * d_sb // nb
        o_fx = _fox_attention(qkv, cum_cols, B, S, tq, d_fx // nb, fx0, fx0 + d_fx // nb,
                              fx0 + 2 * d_fx // nb)
        o_cv = _conv_module(rest, conv_w[l], conv_b[l], conv_ln_g[l], conv_ln_b[l], B, S)

        rw = jnp.concatenate(
            [r1_w[l], r2_w[l].transpose(1, 0, 2).reshape(D, n_experts),
             jnp.zeros((D, LANES - n_groups - n_experts), F32)], axis=1)
        rw_hi = rw.astype(BF16)
        rw_lo = (rw - rw_hi.astype(F32)).astype(BF16)
        rb = jnp.zeros((1, LANES), F32).at[0, :n_groups].set(r1_b[l])
        rb = rb.at[0, n_groups:n_groups + n_experts].set(r2_b[l].reshape(-1))
        x1, h2, logits = _outproj(
            o_sb, o_fx, o_cv, w_out[l].astype(BF16), x2d, gt1,
            ln1_g[l].reshape(1, D), ln1_b[l].reshape(1, D), sc2, sh2, rw_hi, rw_lo, rb, S, alpha)

        e_lanes, w_lanes = _route(logits, n_groups, epg)
        dest_lanes, be_lanes = _plan(e_lanes, n_experts, n_blocks)
        dest_flat = dest_lanes[:, :2].reshape(-1)
        be = be_lanes[:n_blocks, 0]
        n_used = jnp.sum((be < n_experts).astype(jnp.int32)).reshape(1)
        block_expert = jnp.minimum(be, n_experts - 1)
        slot_tok = _slot_map(dest_flat, n_slots)
        yb = _experts(block_expert, n_used, slot_tok, h2, wg16, wu16, wd16, l, n_blocks)
        x2d = _combine(dest_flat, yb, w_lanes, x1, gt2,
                       ln2_g[l].reshape(1, D), ln2_b[l].reshape(1, D), S, alpha)
    return x2d.reshape(B, S, D)
```

```python
import functools

import jax
import jax.numpy as jnp
from jax import lax
from jax.experimental import pallas as pl
from jax.experimental.pallas import tpu as pltpu

LN_EPS = 1e-5
HEAD_DIM = 64
LANES = 128
SUBLANES = 8
MOE_BLOCK = 256
NEG_BIG = -1e30
SB_DEAD_LOG = -104.0
FOX_DEAD_LOG = -104.0
FOX_PAIRS = 2
VMEM_LIMIT = 56 * 1024 * 1024

F32 = jnp.float32
BF16 = jnp.bfloat16


def _pick(n, cands):
    for c in cands:
        if n % c == 0:
            return c
    return n


def _params(sem):
    return pltpu.CompilerParams(dimension_semantics=sem, vmem_limit_bytes=VMEM_LIMIT)


def _ln_rows(v):
    mu = jnp.mean(v, axis=-1, keepdims=True)
    d = v - mu
    var = jnp.mean(d * d, axis=-1, keepdims=True)
    return d * lax.rsqrt(var + LN_EPS)


def _log_sigmoid(z):
    return jnp.minimum(z, 0.0) - jnp.log1p(jnp.exp(-jnp.abs(z)))


def _sigmoid(z):
    return 1.0 / (1.0 + jnp.exp(-z))


def _split2(v):
    hi = v.astype(BF16)
    lo = (v - hi.astype(F32)).astype(BF16)
    return hi, lo


def _dot(a, b):
    return jnp.dot(a, b, preferred_element_type=F32)


def _dot_nt(a, b):
    return lax.dot_general(a, b, (((1,), (1,)), ((), ())), preferred_element_type=F32)


def _ada_kernel(c_ref, w_ref, b_ref, o_ref):
    c = c_ref[...]
    s = c * _sigmoid(c)
    s_hi, s_lo = _split2(s)
    w_hi, w_lo = _split2(w_ref[0])
    acc = _dot(s_hi, w_hi) + _dot(s_lo, w_hi) + _dot(s_hi, w_lo)
    o_ref[0] = acc + b_ref[0]


def _ada(c_pad, ada_w, ada_b):
    L, D, N = ada_w.shape
    rows = c_pad.shape[0]
    tn = _pick(N, (512, 256, 128))
    return pl.pallas_call(
        _ada_kernel,
        grid=(L, N // tn),
        in_specs=[
            pl.BlockSpec((rows, D), lambda l, n: (0, 0)),
            pl.BlockSpec((1, D, tn), lambda l, n: (l, 0, n)),
            pl.BlockSpec((1, 1, tn), lambda l, n: (l, 0, n)),
        ],
        out_specs=pl.BlockSpec((1, rows, tn), lambda l, n: (l, 0, n)),
        out_shape=jax.ShapeDtypeStruct((L, rows, N), F32),
        compiler_params=_params(("arbitrary", "arbitrary")),
        name="ada_mod",
    )(c_pad, ada_w, ada_b.reshape(L, 1, N))


def _lnmod_kernel(x_ref, sc_ref, sh_ref, o_ref):
    h = _ln_rows(x_ref[...]) * (1.0 + sc_ref[0]) + sh_ref[0]
    o_ref[...] = h.astype(o_ref.dtype)


def _lnmod(x2d, sc, sh, seq):
    T, D = x2d.shape
    tm = _pick(seq, (512, 256, 128))
    per_b = seq // tm
    return pl.pallas_call(
        _lnmod_kernel,
        grid=(T // tm,),
        in_specs=[
            pl.BlockSpec((tm, D), lambda i: (i, 0)),
            pl.BlockSpec((1, 1, D), lambda i: (i // per_b, 0, 0)),
            pl.BlockSpec((1, 1, D), lambda i: (i // per_b, 0, 0)),
        ],
        out_specs=pl.BlockSpec((tm, D), lambda i: (i, 0)),
        out_shape=jax.ShapeDtypeStruct((T, D), BF16),
        compiler_params=_params(("arbitrary",)),
        name="ln_mod",
    )(x2d, sc, sh)


def _mm_kernel(a_ref, b_ref, o_ref):
    o_ref[...] = _dot(a_ref[...], b_ref[...]).astype(o_ref.dtype)


def _matmul(a, w, out_dtype, name):
    M, K = a.shape
    N = w.shape[1]
    tm = _pick(M, (1024, 512, 256, 128))
    tn = _pick(N, (512, 384, 256, 128))
    return pl.pallas_call(
        _mm_kernel,
        grid=(M // tm, N // tn),
        in_specs=[
            pl.BlockSpec((tm, K), lambda i, j: (i, 0)),
            pl.BlockSpec((K, tn), lambda i, j: (0, j)),
        ],
        out_specs=pl.BlockSpec((tm, tn), lambda i, j: (i, j)),
        out_shape=jax.ShapeDtypeStruct((M, N), out_dtype),
        compiler_params=_params(("arbitrary", "arbitrary")),
        name=name,
    )(a, w)


def _cum_kernel(f_ref, b_ref, o_ref, carry_ref):
    @pl.when(pl.program_id(1) == 0)
    def _():
        carry_ref[...] = jnp.zeros_like(carry_ref)

    ts = f_ref.shape[0]
    lf = _log_sigmoid(f_ref[...] + b_ref[...])
    p1 = lf.astype(BF16)
    r1 = lf - p1.astype(F32)
    p2 = r1.astype(BF16)
    p3 = (r1 - p2.astype(F32)).astype(BF16)
    row = lax.broadcasted_iota(jnp.int32, (ts, ts), 0)
    col = lax.broadcasted_iota(jnp.int32, (ts, ts), 1)
    tri = jnp.where(col <= row, 1.0, 0.0).astype(BF16)
    cum = _dot(tri, p1) + _dot(tri, p2) + _dot(tri, p3) + carry_ref[...]
    o_ref[...] = cum
    carry_ref[...] = cum[ts - 1:ts, :]


def _forget_cumsum(rest, b_pad, batch, seq, col_block):
    T = rest.shape[0]
    ts = _pick(seq, (256, 128))
    ns = seq // ts
    return pl.pallas_call(
        _cum_kernel,
        grid=(batch, ns),
        in_specs=[
            pl.BlockSpec((ts, LANES), lambda b, s: (b * ns + s, col_block)),
            pl.BlockSpec((1, LANES), lambda b, s: (0, 0)),
        ],
        out_specs=pl.BlockSpec((ts, LANES), lambda b, s: (b * ns + s, 0)),
        out_shape=jax.ShapeDtypeStruct((T, LANES), F32),
        scratch_shapes=[pltpu.VMEM((1, LANES), F32)],
        compiler_params=_params(("arbitrary", "arbitrary")),
        name="forget_cumsum",
    )(rest, b_pad)


def _masked_heads(q_ref, pairs):
    lane = lax.broadcasted_iota(jnp.int32, (1, LANES), 1)
    heads = []
    for p in range(pairs):
        q2 = q_ref[:, p * LANES:(p + 1) * LANES] * jnp.asarray(HEAD_DIM ** -0.5, q_ref.dtype)
        zero = jnp.zeros_like(q2)
        heads += [jnp.where(lane < HEAD_DIM, q2, zero), jnp.where(lane >= HEAD_DIM, q2, zero)]
    return heads


def _pairs_per_step(cands, *counts):
    return next(c for c in cands if all(n % c == 0 for n in counts))


def _store_value_transposed(v_ref, vt_ref, first):
    pairs, nk, _, two_tk = vt_ref.shape
    tk = two_tk // 2
    for p in range(pairs):
        for j in range(nk):
            vt = v_ref[j * tk:(j + 1) * tk, p * LANES:(p + 1) * LANES].astype(F32).T
            both = jnp.concatenate([jnp.where(first, vt, 0.0), jnp.where(first, 0.0, vt)], axis=1)
            vt_ref[p, j] = both.astype(BF16)


def _sb_kernel(q_ref, k_ref, v_ref, o_ref, vt_ref, *, pairs):
    tq = q_ref.shape[0]
    i = pl.program_id(2)
    q_heads = _masked_heads(q_ref, pairs)
    first = lax.broadcasted_iota(jnp.int32, (LANES, 1), 0) < HEAD_DIM

    @pl.when(i == 0)
    def _():
        _store_value_transposed(v_ref, vt_ref, first)

    key = lax.broadcasted_iota(jnp.int32, (tq, tq), 0)
    qry = lax.broadcasted_iota(jnp.int32, (tq, tq), 1)
    later_keys = jnp.where(qry > key, 1.0, 0.0).astype(BF16)

    def block(j, state, diagonal):
        start = pl.multiple_of(j * tq, tq)
        out = []
        for p in range(pairs):
            kblk = k_ref[pl.ds(start, tq), p * LANES:(p + 1) * LANES]
            weights, runs = [], []
            for s in range(2):
                run = state[3 * p + s]
                z = _dot_nt(kblk, q_heads[2 * p + s])
                ls = jnp.minimum(z, 0.0) - jnp.log(1.0 + jnp.exp(-jnp.abs(z)))
                lk = ls - z
                if diagonal:
                    lk = jnp.where(key < qry, lk, 0.0)
                lk_hi, lk_lo = _split2(lk)
                later = _dot(later_keys, lk_hi) + _dot(later_keys, lk_lo)
                w = jnp.exp(ls + later + run)
                if diagonal:
                    w = jnp.where(key < qry, w, 0.0)
                weights.append(w.astype(BF16))
                runs.append(run + jnp.sum(lk, axis=0, keepdims=True))
            acc = state[3 * p + 2] + _dot(vt_ref[p, j], jnp.concatenate(weights, axis=0))
            out += [runs[0], runs[1], acc]
        return tuple(out)

    def live(state):
        top = state[0]
        for p in range(pairs):
            top = jnp.maximum(top, jnp.maximum(state[3 * p], state[3 * p + 1]))
        return (jnp.max(top) > SB_DEAD_LOG).astype(jnp.int32)

    init = (jnp.zeros((1, tq), F32), jnp.zeros((1, tq), F32), jnp.zeros((LANES, tq), F32)) * pairs
    state = block(i, init, True)

    def cond(carry):
        return (carry[0] >= 0) & (carry[1] > 0)

    def body(carry):
        state = block(carry[0], carry[2:], False)
        return (carry[0] - 1, live(state)) + state

    res = lax.while_loop(cond, body, (i - 1, live(state)) + state)
    slabs = [res[2 + 3 * p + 2].T for p in range(pairs)]
    o = slabs[0] if pairs == 1 else jnp.concatenate(slabs, axis=1)
    o_ref[...] = o.astype(o_ref.dtype)


def _sb_attention(qkv, batch, seq, n_pairs, q_col, k_col, v_col):
    T = qkv.shape[0]
    tq = _pick(seq, (256, 128))
    nq = seq // tq
    pairs = _pairs_per_step((4, 2, 1), n_pairs, q_col, k_col, v_col)
    W = pairs * LANES
    qc, kc, vc = q_col // pairs, k_col // pairs, v_col // pairs
    return pl.pallas_call(
        functools.partial(_sb_kernel, pairs=pairs),
        grid=(batch, n_pairs // pairs, nq),
        in_specs=[
            pl.BlockSpec((tq, W), lambda b, p, i: (b * nq + i, qc + p)),
            pl.BlockSpec((seq, W), lambda b, p, i: (b, kc + p)),
            pl.BlockSpec((seq, W), lambda b, p, i: (b, vc + p)),
        ],
        out_specs=pl.BlockSpec((tq, W), lambda b, p, i: (b * nq + i, p)),
        out_shape=jax.ShapeDtypeStruct((T, n_pairs * LANES), BF16),
        scratch_shapes=[pltpu.VMEM((pairs, nq, LANES, 2 * tq), BF16)],
        compiler_params=_params(("arbitrary", "arbitrary", "arbitrary")),
        name="sb_attention",
    )(qkv, qkv, qkv)


def _fox_kernel(q_ref, k_ref, v_ref, cc_ref, o_ref, vt_ref, kn_ref, *, pairs):
    tq = q_ref.shape[0]
    nk = vt_ref.shape[1]
    i = pl.program_id(2)
    q_heads = _masked_heads(q_ref, pairs)
    first = lax.broadcasted_iota(jnp.int32, (LANES, 1), 0) < HEAD_DIM
    lane = lax.broadcasted_iota(jnp.int32, (1, LANES), 1)

    @pl.when(i == 0)
    def _():
        _store_value_transposed(v_ref, vt_ref, first)
        for p in range(pairs):
            top = [jnp.zeros((1, 1), F32), jnp.zeros((1, 1), F32)]
            for j in range(nk):
                kf = k_ref[j * tq:(j + 1) * tq, p * LANES:(p + 1) * LANES].astype(F32)
                sq = kf * kf
                for s in range(2):
                    own = (lane < HEAD_DIM) if s == 0 else (lane >= HEAD_DIM)
                    norm2 = jnp.sum(jnp.where(own, sq, 0.0), axis=1, keepdims=True)
                    top[s] = jnp.maximum(top[s], jnp.max(norm2, axis=0, keepdims=True))
            for s in range(2):
                kn_ref[2 * p + s] = jnp.broadcast_to(jnp.sqrt(top[s]), kn_ref.shape[1:])

    ones = jnp.ones((SUBLANES, LANES), BF16)
    bound = []
    for h in range(2 * pairs):
        qf = q_heads[h].astype(F32)
        qn2 = _dot_nt(ones, (qf * qf).astype(BF16))[0:1, :]
        bound.append(jnp.sqrt(qn2) * kn_ref[h][0:1, 0:1] * 1.02 + 1e-3)

    key = lax.broadcasted_iota(jnp.int32, (tq, tq), 0)
    qry = lax.broadcasted_iota(jnp.int32, (tq, tq), 1)

    def block(j, carry, diagonal):
        start = pl.multiple_of(j * tq, tq)
        out = []
        for p in range(pairs):
            kblk = k_ref[pl.ds(start, tq), p * LANES:(p + 1) * LANES]
            acc = carry[5 * p + 4]
            stats = []
            for s in range(2):
                m, l = carry[5 * p + 2 * s], carry[5 * p + 2 * s + 1]
                logits = _dot_nt(kblk, q_heads[2 * p + s]) - cc_ref[p, pl.ds(start, tq), s:s + 1]
                if diagonal:
                    logits = jnp.where(key <= qry, logits, NEG_BIG)
                m_new = jnp.maximum(m, jnp.max(logits, axis=0, keepdims=True))
                alpha = jnp.exp(m - m_new)
                prob = jnp.exp(logits - m_new)
                l = l * alpha + jnp.sum(prob, axis=0, keepdims=True)
                stats.append((m_new, l, alpha, prob.astype(BF16)))
            pv = _dot(vt_ref[p, j], jnp.concatenate([stats[0][3], stats[1][3]], axis=0))
            acc = acc * jnp.where(first, stats[0][2], stats[1][2]) + pv
            out += [stats[0][0], stats[0][1], stats[1][0], stats[1][1], acc]
        return tuple(out)

    row_stat = (jnp.full((1, tq), NEG_BIG, F32), jnp.zeros((1, tq), F32))
    init = (row_stat * 2 + (jnp.zeros((LANES, tq), F32),)) * pairs
    def live(state, j):
        last = jnp.maximum(j, 0) * tq + (tq - 1)
        top = jnp.full((1, tq), NEG_BIG, F32)
        for h in range(2 * pairs):
            gate = cc_ref[h // 2, pl.ds(last, 1), (h % 2):(h % 2) + 1]
            top = jnp.maximum(top, bound[h] - gate - state[5 * (h // 2) + 2 * (h % 2)])
        return (jnp.max(top) > FOX_DEAD_LOG).astype(jnp.int32)

    state = block(i, init, True)

    def cond(carry):
        return (carry[0] >= 0) & (carry[1] > 0)

    def body(carry):
        state = block(carry[0], carry[2:], False)
        return (carry[0] - 1, live(state, carry[0] - 1)) + state

    res = lax.while_loop(cond, body, (i - 1, live(state, i - 1)) + state)[2:]
    slabs = []
    for p in range(pairs):
        inv = jnp.where(first, 1.0 / res[5 * p + 1], 1.0 / res[5 * p + 3])
        slabs.append((res[5 * p + 4] * inv).T)
    o = slabs[0] if pairs == 1 else jnp.concatenate(slabs, axis=1)
    o_ref[...] = o.astype(o_ref.dtype)


def _fox_attention(qkv, cum_cols, batch, seq, tq, n_pairs, q_col, k_col, v_col):
    T = qkv.shape[0]
    nq = seq // tq
    pairs = _pairs_per_step((FOX_PAIRS, 1), n_pairs, q_col, k_col, v_col)
    W = pairs * LANES
    qc, kc, vc = q_col // pairs, k_col // pairs, v_col // pairs
    return pl.pallas_call(
        functools.partial(_fox_kernel, pairs=pairs),
        grid=(batch, n_pairs // pairs, nq),
        in_specs=[
            pl.BlockSpec((tq, W), lambda b, p, i: (b * nq + i, qc + p)),
            pl.BlockSpec((seq, W), lambda b, p, i: (b, kc + p)),
            pl.BlockSpec((seq, W), lambda b, p, i: (b, vc + p)),
            pl.BlockSpec((pairs, seq, 2), lambda b, p, i: (p, b, 0)),
        ],
        out_specs=pl.BlockSpec((tq, W), lambda b, p, i: (b * nq + i, p)),
        out_shape=jax.ShapeDtypeStruct((T, n_pairs * LANES), BF16),
        scratch_shapes=[pltpu.VMEM((pairs, nq, LANES, 2 * tq), BF16),
                        pltpu.VMEM((2 * pairs, SUBLANES, LANES), F32)],
        compiler_params=_params(("arbitrary", "arbitrary", "arbitrary")),
        name="fox_attention",
    )(qkv, qkv, qkv, cum_cols)


def _conv_kernel(a_ref, g_ref, w_ref, cb_ref, lg_ref, lb_ref, o_ref, u_ref, *, width, halo):
    ts = a_ref.shape[0]

    @pl.when(pl.program_id(1) == 0)
    def _():
        u_ref[0:halo, :] = jnp.zeros((halo, u_ref.shape[1]), F32)

    u_ref[halo:halo + ts, :] = a_ref[...] * _sigmoid(g_ref[...])
    acc = jnp.zeros(a_ref.shape, F32) + cb_ref[...]
    for k in range(width):
        off = halo - (width - 1) + k
        acc = acc + w_ref[k:k + 1, :] * u_ref[off:off + ts, :]
    y = _ln_rows(acc) * lg_ref[...] + lb_ref[...]
    o_ref[...] = (y * _sigmoid(y)).astype(o_ref.dtype)
    u_ref[0:halo, :] = u_ref[ts:ts + halo, :]


def _conv_module(rest, conv_w, conv_b, ln_g, ln_b, batch, seq):
    T = rest.shape[0]
    width, C = conv_w.shape
    halo = -(-(width - 1) // SUBLANES) * SUBLANES
    ts = _pick(seq, (128,))
    ns = seq // ts
    kern = functools.partial(_conv_kernel, width=width, halo=halo)
    vec = lambda: pl.BlockSpec((1, C), lambda b, s: (0, 0))
    return pl.pallas_call(
        kern,
        grid=(batch, ns),
        in_specs=[
            pl.BlockSpec((ts, C), lambda b, s: (b * ns + s, 0)),
            pl.BlockSpec((ts, C), lambda b, s: (b * ns + s, 1)),
            pl.BlockSpec((width, C), lambda b, s: (0, 0)),
            vec(), vec(), vec(),
        ],
        out_specs=pl.BlockSpec((ts, C), lambda b, s: (b * ns + s, 0)),
        out_shape=jax.ShapeDtypeStruct((T, C), BF16),
        scratch_shapes=[pltpu.VMEM((ts + halo, C), F32)],
        compiler_params=_params(("arbitrary", "arbitrary")),
        name="conformer_conv",
    )(rest, rest, conv_w, conv_b.reshape(1, C), ln_g.reshape(1, C), ln_b.reshape(1, C))


def _outproj_kernel(osb_ref, ofx_ref, ocv_ref, w_ref, x_ref, gt_ref, g_ref, b_ref,
                    sc_ref, sh_ref, rwh_ref, rwl_ref, rb_ref,
                    x1_ref, h2_ref, lg_ref, *, alpha):
    d_sb = osb_ref.shape[1]
    d_fx = ofx_ref.shape[1]
    y = _dot(osb_ref[...], w_ref[0:d_sb, :])
    y = y + _dot(ofx_ref[...], w_ref[d_sb:d_sb + d_fx, :])
    y = y + _dot(ocv_ref[...], w_ref[d_sb + d_fx:, :])
    x1 = _ln_rows(alpha * x_ref[...] + gt_ref[0] * y) * g_ref[...] + b_ref[...]
    x1_ref[...] = x1
    h2 = _ln_rows(x1) * (1.0 + sc_ref[0]) + sh_ref[0]
    h2_ref[...] = h2
    h_hi, h_lo = _split2(h2)
    lg = _dot(h_hi, rwh_ref[...]) + _dot(h_lo, rwh_ref[...]) + _dot(h_hi, rwl_ref[...])
    lg_ref[...] = lg + rb_ref[...]


def _outproj(o_sb, o_fx, o_cv, w_out, x2d, gt, g, b, sc2, sh2, rw_hi, rw_lo, rb, seq, alpha):
    T, D = x2d.shape
    tm = _pick(seq, (256, 128))
    per_b = seq // tm
    rowblk = lambda n: pl.BlockSpec((tm, n), lambda i: (i, 0))
    full = lambda r, c: pl.BlockSpec((r, c), lambda i: (0, 0))
    perb = lambda: pl.BlockSpec((1, 1, D), lambda i: (i // per_b, 0, 0))
    return pl.pallas_call(
        functools.partial(_outproj_kernel, alpha=alpha),
        grid=(T // tm,),
        in_specs=[
            rowblk(o_sb.shape[1]), rowblk(o_fx.shape[1]), rowblk(o_cv.shape[1]),
            full(D, D), rowblk(D), perb(), full(1, D), full(1, D), perb(), perb(),
            full(D, LANES), full(D, LANES), full(1, LANES),
        ],
        out_specs=[rowblk(D), rowblk(D), rowblk(LANES)],
        out_shape=[jax.ShapeDtypeStruct((T, D), F32), jax.ShapeDtypeStruct((T, D), F32),
                   jax.ShapeDtypeStruct((T, LANES), F32)],
        compiler_params=_params(("arbitrary",)),
        name="out_proj_norm_router",
    )(o_sb, o_fx, o_cv, w_out, x2d, gt, g, b, sc2, sh2, rw_hi, rw_lo, rb)


def _route_kernel(lg_ref, e_ref, w_ref, *, n_groups, epg):
    L = lg_ref[...]
    lane_i = lax.broadcasted_iota(jnp.int32, L.shape, 1)
    lane = lane_i.astype(F32)
    far = 1e6
    is_grp = lane < n_groups
    l1 = jnp.where(is_grp, L, NEG_BIG)
    m1 = jnp.max(l1, axis=1, keepdims=True)
    grp = jnp.min(jnp.where(is_grp & (l1 == m1), lane, far), axis=1, keepdims=True)
    s1 = jnp.sum(jnp.where(is_grp, jnp.exp(l1 - m1), 0.0), axis=1, keepdims=True)
    p_grp = 1.0 / s1
    lo = n_groups + grp * epg
    in_grp = (lane >= lo) & (lane < lo + epg)
    l2 = jnp.where(in_grp, L, NEG_BIG)
    m2 = jnp.max(l2, axis=1, keepdims=True)
    e2 = jnp.where(in_grp, jnp.exp(l2 - m2), -1.0)
    v1 = jnp.max(e2, axis=1, keepdims=True)
    i1 = jnp.min(jnp.where(e2 == v1, lane, far), axis=1, keepdims=True)
    e2b = jnp.where(lane == i1, -1.0, e2)
    v2 = jnp.max(e2b, axis=1, keepdims=True)
    i2 = jnp.min(jnp.where(e2b == v2, lane, far), axis=1, keepdims=True)
    den = v1 + v2
    w0 = p_grp * v1 / den
    w1 = p_grp * v2 / den
    ids = jnp.where(lane_i == 0, i1 - n_groups, jnp.where(lane_i == 1, i2 - n_groups, 0.0))
    e_ref[...] = ids.astype(jnp.int32)
    w_ref[...] = jnp.where(lane_i == 0, w0, jnp.where(lane_i == 1, w1, 0.0))


def _route(logits, n_groups, epg):
    T = logits.shape[0]
    tm = _pick(T, (512, 256, 128))
    blk = lambda: pl.BlockSpec((tm, LANES), lambda i: (i, 0))
    return pl.pallas_call(
        functools.partial(_route_kernel, n_groups=n_groups, epg=epg),
        grid=(T // tm,),
        in_specs=[blk()],
        out_specs=[blk(), blk()],
        out_shape=[jax.ShapeDtypeStruct((T, LANES), jnp.int32),
                   jax.ShapeDtypeStruct((T, LANES), F32)],
        compiler_params=_params(("arbitrary",)),
        name="route_topk",
    )(logits)


def _plan_kernel(e_ref, dest_ref, be_ref, cnt_ref, base_ref, *, n_experts):
    p = pl.program_id(0)
    i = pl.program_id(1)
    tm = e_ref.shape[0]
    lane = lax.broadcasted_iota(jnp.int32, (tm, LANES), 1)
    e = e_ref[...]
    hot0 = lane == e[:, 0:1]
    hot1 = lane == e[:, 1:2]
    both = jnp.where(hot0 | hot1, 1.0, 0.0)

    @pl.when((p == 0) & (i == 0))
    def _():
        cnt_ref[...] = jnp.zeros_like(cnt_ref)

    @pl.when(p == 0)
    def _():
        cnt_ref[...] += jnp.sum(both, axis=0, keepdims=True)

    @pl.when((p == 1) & (i == 0))
    def _():
        nblk = jnp.floor((cnt_ref[...] + (MOE_BLOCK - 1)) * (1.0 / MOE_BLOCK))
        r = lax.broadcasted_iota(jnp.int32, (LANES, LANES), 0)
        c = lax.broadcasted_iota(jnp.int32, (LANES, LANES), 1)
        before = jnp.where(r < c, 1.0, 0.0).astype(BF16)
        nb8 = jnp.broadcast_to(nblk, (SUBLANES, LANES)).astype(BF16)
        excl = _dot(nb8, before)[0:1, :]
        base_ref[...] = excl * MOE_BLOCK
        cnt_ref[...] = jnp.zeros_like(cnt_ref)
        incl = excl + nblk
        nb_rows = be_ref.shape[0]
        bidx = lax.broadcasted_iota(jnp.int32, (nb_rows, LANES), 0).astype(F32)
        lane_b = lax.broadcasted_iota(jnp.int32, (nb_rows, LANES), 1)
        done = jnp.where((incl <= bidx) & (lane_b < n_experts), 1.0, 0.0)
        be = jnp.sum(done, axis=1, keepdims=True)
        be_ref[...] = jnp.broadcast_to(be, (nb_rows, LANES)).astype(jnp.int32)

    @pl.when(p == 1)
    def _():
        r = lax.broadcasted_iota(jnp.int32, (tm, tm), 0)
        c = lax.broadcasted_iota(jnp.int32, (tm, tm), 1)
        earlier = jnp.where(c < r, 1.0, 0.0).astype(BF16)
        tot = _dot(earlier, both.astype(BF16)) + cnt_ref[...] + base_ref[...]
        d0 = jnp.sum(jnp.where(hot0, tot, 0.0), axis=1, keepdims=True)
        d1 = jnp.sum(jnp.where(hot1, tot, 0.0), axis=1, keepdims=True)
        dest = jnp.where(lane == 0, d0, jnp.where(lane == 1, d1, 0.0))
        dest_ref[...] = dest.astype(jnp.int32)
        cnt_ref[...] += jnp.sum(both, axis=0, keepdims=True)


def _plan(e_lanes, n_experts, n_blocks):
    T = e_lanes.shape[0]
    tm = _pick(T, (256, 128))
    nb_rows = -(-n_blocks // SUBLANES) * SUBLANES
    return pl.pallas_call(
        functools.partial(_plan_kernel, n_experts=n_experts),
        grid=(2, T // tm),
        in_specs=[pl.BlockSpec((tm, LANES), lambda p, i: (i, 0))],
        out_specs=[pl.BlockSpec((tm, LANES), lambda p, i: (i * p, 0)),
                   pl.BlockSpec((nb_rows, LANES), lambda p, i: (0, 0))],
        out_shape=[jax.ShapeDtypeStruct((T, LANES), jnp.int32),
                   jax.ShapeDtypeStruct((nb_rows, LANES), jnp.int32)],
        scratch_shapes=[pltpu.VMEM((1, LANES), F32), pltpu.VMEM((1, LANES), F32)],
        compiler_params=_params(("arbitrary", "arbitrary")),
        name="dispatch_plan",
    )(e_lanes)


def _row_copy(src, src_row, dst, dst_row, sem):
    return pltpu.make_async_copy(src.at[pl.ds(src_row, 1)], dst.at[pl.ds(dst_row, 1)], sem)


def _slotmap_kernel(dest_ref, tok_ref):
    def clear(s, carry):
        tok_ref[s] = 0
        return carry

    lax.fori_loop(0, tok_ref.shape[0], clear, 0, unroll=8)

    def put(t, carry):
        tok_ref[dest_ref[2 * t]] = t
        tok_ref[dest_ref[2 * t + 1]] = t
        return carry

    lax.fori_loop(0, dest_ref.shape[0] // 2, put, 0, unroll=8)


def _slot_map(dest_flat, n_slots):
    return pl.pallas_call(
        _slotmap_kernel,
        in_specs=[pl.BlockSpec(memory_space=pltpu.SMEM)],
        out_specs=pl.BlockSpec(memory_space=pltpu.SMEM),
        out_shape=jax.ShapeDtypeStruct((n_slots,), jnp.int32),
        name="moe_slot_map",
    )(dest_flat)


def _expert_kernel(be_ref, nu_ref, tok_ref, h_ref, wg_ref, wu_ref, wd_ref, o_ref, xbuf, sems):
    del be_ref
    b = pl.program_id(0)
    n_used = nu_ref[0]
    slot = b % 2

    def issue(blk, s, r):
        _row_copy(h_ref, tok_ref[blk * MOE_BLOCK + r], xbuf.at[s], r, sems.at[s]).start()

    @pl.when(b == 0)
    def _():
        def step(r, carry):
            issue(0, 0, r)
            return carry

        lax.fori_loop(0, MOE_BLOCK, step, 0, unroll=8)

    @pl.when(b + 1 < n_used)
    def _():
        for r in range(MOE_BLOCK):
            issue(b + 1, 1 - slot, r)

    @pl.when(b < n_used)
    def _():
        def drain(r, carry):
            _row_copy(h_ref, 0, xbuf.at[slot], 0, sems.at[slot]).wait()
            return carry

        lax.fori_loop(0, MOE_BLOCK, drain, 0, unroll=8)
        x = xbuf[slot].astype(BF16)
        g = _dot(x, wg_ref[0, 0])
        u = _dot(x, wu_ref[0, 0])
        hid = (g * _sigmoid(g)) * u
        o_ref[...] = _dot(hid.astype(BF16), wd_ref[0, 0])

    @pl.when(b >= n_used)
    def _():
        o_ref[...] = jnp.zeros_like(o_ref)


def _experts(block_expert, n_used, slot_tok, h2, w_gate, w_up, w_down, layer, n_blocks):
    D = h2.shape[1]
    DE = w_gate.shape[3]
    wmap = lambda b, be, nu, tok: (layer, be[jnp.minimum(b, nu[0] - 1)], 0, 0)
    return pl.pallas_call(
        _expert_kernel,
        grid_spec=pltpu.PrefetchScalarGridSpec(
            num_scalar_prefetch=3,
            grid=(n_blocks,),
            in_specs=[
                pl.BlockSpec(memory_space=pl.ANY),
                pl.BlockSpec((1, 1, D, DE), wmap),
                pl.BlockSpec((1, 1, D, DE), wmap),
                pl.BlockSpec((1, 1, DE, D), wmap),
            ],
            out_specs=pl.BlockSpec((MOE_BLOCK, D), lambda b, be, nu, tok: (b, 0)),
            scratch_shapes=[pltpu.VMEM((2, MOE_BLOCK, D), F32), pltpu.SemaphoreType.DMA((2,))],
        ),
        out_shape=jax.ShapeDtypeStruct((n_blocks * MOE_BLOCK, D), F32),
        compiler_params=_params(("arbitrary",)),
        name="moe_experts",
    )(block_expert, n_used, slot_tok, h2, w_gate, w_up, w_down)


def _combine_kernel(dest_ref, yb_ref, w_ref, x_ref, gt_ref, g_ref, b_ref, o_ref,
                    buf_ref, sems, *, alpha):
    tm = x_ref.shape[0]
    i = pl.program_id(0)
    slot = i % 2

    def issue(tile, s, r):
        t = tile * tm + r
        _row_copy(yb_ref, dest_ref[2 * t], buf_ref.at[s, 0], r, sems.at[s]).start()
        _row_copy(yb_ref, dest_ref[2 * t + 1], buf_ref.at[s, 1], r, sems.at[s]).start()

    @pl.when(i == 0)
    def _():
        def step(r, carry):
            issue(0, 0, r)
            return carry

        lax.fori_loop(0, tm, step, 0, unroll=8)

    @pl.when(i + 1 < pl.num_programs(0))
    def _():
        for r in range(tm):
            issue(i + 1, 1 - slot, r)

    def drain(r, carry):
        _row_copy(yb_ref, 0, buf_ref.at[slot, 0], 0, sems.at[slot]).wait()
        return carry

    lax.fori_loop(0, 2 * tm, drain, 0, unroll=8)
    w = w_ref[...]
    y = w[:, 0:1] * buf_ref[slot, 0] + w[:, 1:2] * buf_ref[slot, 1]
    v = alpha * x_ref[...] + gt_ref[0] * y
    o_ref[...] = _ln_rows(v) * g_ref[...] + b_ref[...]


def _combine(dest_flat, yb, w_lanes, x1, gt, g, b, seq, alpha):
    T, D = x1.shape
    tm = _pick(seq, (256, 128))
    per_b = seq // tm
    return pl.pallas_call(
        functools.partial(_combine_kernel, alpha=alpha),
        grid_spec=pltpu.PrefetchScalarGridSpec(
            num_scalar_prefetch=1,
            grid=(T // tm,),
            in_specs=[
                pl.BlockSpec(memory_space=pl.ANY),
                pl.BlockSpec((tm, LANES), lambda i, d: (i, 0)),
                pl.BlockSpec((tm, D), lambda i, d: (i, 0)),
                pl.BlockSpec((1, 1, D), lambda i, d: (i // per_b, 0, 0)),
                pl.BlockSpec((1, D), lambda i, d: (0, 0)),
                pl.BlockSpec((1, D), lambda i, d: (0, 0)),
            ],
            out_specs=pl.BlockSpec((tm, D), lambda i, d: (i, 0)),
            scratch_shapes=[pltpu.VMEM((2, 2, tm, D), F32), pltpu.SemaphoreType.DMA((2,))],
        ),
        out_shape=jax.ShapeDtypeStruct((T, D), F32),
        compiler_params=_params(("arbitrary",)),
        name="moe_combine_norm",
    )(dest_flat, yb, w_lanes, x1, gt, g, b)


def kernel(x, c, ada_w, ada_b, w_in, b_forget, conv_w, conv_b, conv_ln_g, conv_ln_b, w_out,
           ln1_g, ln1_b, r1_w, r1_b, r2_w, r2_b, w_gate, w_up, w_down, ln2_g, ln2_b):
    B, S, D = x.shape
    L = ada_w.shape[0]
    T = B * S
    alpha = float((2 * L) ** 0.25)
    d_sb, d_fx, c_cv = D // 4, D // 2, D // 4
    n_fx = d_fx // HEAD_DIM
    n_groups = r1_w.shape[-1]
    epg = r2_w.shape[-1]
    n_experts = n_groups * epg
    n_blocks = (2 * T) // MOE_BLOCK + n_experts
    n_slots = n_blocks * MOE_BLOCK
    qkv_cols = 3 * d_sb + 3 * d_fx
    tq = _pick(S, (512, 256, 128))

    rows = -(-B // SUBLANES) * SUBLANES
    c_pad = jnp.zeros((rows, D), F32).at[:B].set(c)
    mod_all = _ada(c_pad, ada_w, ada_b)

    wg16, wu16, wd16 = w_gate.astype(BF16), w_up.astype(BF16), w_down.astype(BF16)
    x2d = x.reshape(T, D)
    for l in range(L):
        mod = mod_all[l, :B]
        sh1, sc1, gt1, sh2, sc2, gt2 = [m.reshape(B, 1, D) for m in jnp.split(mod, 6, axis=-1)]

        h1 = _lnmod(x2d, sc1, sh1, S)
        w_qkv = w_in[l, :, :qkv_cols].astype(BF16)
        f_lo = qkv_cols
        g_lo = qkv_cols + n_fx
        w_rest = jnp.concatenate(
            [w_in[l, :, g_lo:g_lo + 2 * c_cv], w_in[l, :, f_lo:f_lo + n_fx],
             jnp.zeros((D, LANES - n_fx), F32)], axis=1).astype(BF16)
        qkv = _matmul(h1, w_qkv, BF16, "proj_qkv")
        rest = _matmul(h1, w_rest, F32, "proj_glu_forget")

        b_pad = jnp.zeros((1, LANES), F32).at[0, :n_fx].set(b_forget[l])
        cum = _forget_cumsum(rest, b_pad, B, S, (2 * c_cv) // LANES)[:, :n_fx]
        cum_cols = cum.reshape(T, n_fx // 2, 2).transpose(1, 0, 2)

        nb = LANES
        o_sb = _sb_attention(qkv, B, S, d_sb // nb, 0, d_sb // nb, 2 * d_sb // nb)
        fx0 = 3 * d_sb // nb
        o_fx = _fox_attention(qkv, cum_cols, B, S, tq, d_fx // nb, fx0, fx0 + d_fx // nb,
                              fx0 + 2 * d_fx // nb)
        o_cv = _conv_module(rest, conv_w[l], conv_b[l], conv_ln_g[l], conv_ln_b[l], B, S)

        rw = jnp.concatenate(
            [r1_w[l], r2_w[l].transpose(1, 0, 2).reshape(D, n_experts),
             jnp.zeros((D, LANES - n_groups - n_experts), F32)], axis=1)
        rw_hi = rw.astype(BF16)
        rw_lo = (rw - rw_hi.astype(F32)).astype(BF16)
        rb = jnp.zeros((1, LANES), F32).at[0, :n_groups].set(r1_b[l])
        rb = rb.at[0, n_groups:n_groups + n_experts].set(r2_b[l].reshape(-1))
        x1, h2, logits = _outproj(
            o_sb, o_fx, o_cv, w_out[l].astype(BF16), x2d, gt1,
            ln1_g[l].reshape(1, D), ln1_b[l].reshape(1, D), sc2, sh2, rw_hi, rw_lo, rb, S, alpha)

        e_lanes, w_lanes = _route(logits, n_groups, epg)
        dest_lanes, be_lanes = _plan(e_lanes, n_experts, n_blocks)
        dest_flat = dest_lanes[:, :2].reshape(-1)
        be = be_lanes[:n_blocks, 0]
        n_used = jnp.sum((be < n_experts).astype(jnp.int32)).reshape(1)
        block_expert = jnp.minimum(be, n_experts - 1)
        slot_tok = _slot_map(dest_flat, n_slots)
        yb = _experts(block_expert, n_used, slot_tok, h2, wg16, wu16, wd16, l, n_blocks)
        x2d = _combine(dest_flat, yb, w_lanes, x1, gt2,
                       ln2_g[l].reshape(1, D), ln2_b[l].reshape(1, D), S, alpha)
    return x2d.reshape(B, S, D)
```

```python
import functools

import jax
import jax.numpy as jnp
from jax import lax
from jax.experimental import pallas as pl
from jax.experimental.pallas import tpu as pltpu

LN_EPS = 1e-5
HEAD_DIM = 64
LANES = 128
SUBLANES = 8
MOE_BLOCK = 256
NEG_BIG = -1e30
SB_DEAD_LOG = -104.0
FOX_PAIRS = 2
VMEM_LIMIT = 56 * 1024 * 1024

F32 = jnp.float32
BF16 = jnp.bfloat16


def _pick(n, cands):
    for c in cands:
        if n % c == 0:
            return c
    return n


def _params(sem):
    return pltpu.CompilerParams(dimension_semantics=sem, vmem_limit_bytes=VMEM_LIMIT)


def _ln_rows(v):
    mu = jnp.mean(v, axis=-1, keepdims=True)
    d = v - mu
    var = jnp.mean(d * d, axis=-1, keepdims=True)
    return d * lax.rsqrt(var + LN_EPS)


def _log_sigmoid(z):
    return jnp.minimum(z, 0.0) - jnp.log1p(jnp.exp(-jnp.abs(z)))


def _sigmoid(z):
    return 1.0 / (1.0 + jnp.exp(-z))


def _split2(v):
    hi = v.astype(BF16)
    lo = (v - hi.astype(F32)).astype(BF16)
    return hi, lo


def _dot(a, b):
    return jnp.dot(a, b, preferred_element_type=F32)


def _dot_nt(a, b):
    return lax.dot_general(a, b, (((1,), (1,)), ((), ())), preferred_element_type=F32)


def _ada_kernel(c_ref, w_ref, b_ref, o_ref):
    c = c_ref[...]
    s = c * _sigmoid(c)
    s_hi, s_lo = _split2(s)
    w_hi, w_lo = _split2(w_ref[0])
    acc = _dot(s_hi, w_hi) + _dot(s_lo, w_hi) + _dot(s_hi, w_lo)
    o_ref[0] = acc + b_ref[0]


def _ada(c_pad, ada_w, ada_b):
    L, D, N = ada_w.shape
    rows = c_pad.shape[0]
    tn = _pick(N, (512, 256, 128))
    return pl.pallas_call(
        _ada_kernel,
        grid=(L, N // tn),
        in_specs=[
            pl.BlockSpec((rows, D), lambda l, n: (0, 0)),
            pl.BlockSpec((1, D, tn), lambda l, n: (l, 0, n)),
            pl.BlockSpec((1, 1, tn), lambda l, n: (l, 0, n)),
        ],
        out_specs=pl.BlockSpec((1, rows, tn), lambda l, n: (l, 0, n)),
        out_shape=jax.ShapeDtypeStruct((L, rows, N), F32),
        compiler_params=_params(("arbitrary", "arbitrary")),
        name="ada_mod",
    )(c_pad, ada_w, ada_b.reshape(L, 1, N))


def _lnmod_kernel(x_ref, sc_ref, sh_ref, o_ref):
    h = _ln_rows(x_ref[...]) * (1.0 + sc_ref[0]) + sh_ref[0]
    o_ref[...] = h.astype(o_ref.dtype)


def _lnmod(x2d, sc, sh, seq):
    T, D = x2d.shape
    tm = _pick(seq, (512, 256, 128))
    per_b = seq // tm
    return pl.pallas_call(
        _lnmod_kernel,
        grid=(T // tm,),
        in_specs=[
            pl.BlockSpec((tm, D), lambda i: (i, 0)),
            pl.BlockSpec((1, 1, D), lambda i: (i // per_b, 0, 0)),
            pl.BlockSpec((1, 1, D), lambda i: (i // per_b, 0, 0)),
        ],
        out_specs=pl.BlockSpec((tm, D), lambda i: (i, 0)),
        out_shape=jax.ShapeDtypeStruct((T, D), BF16),
        compiler_params=_params(("arbitrary",)),
        name="ln_mod",
    )(x2d, sc, sh)


def _mm_kernel(a_ref, b_ref, o_ref):
    o_ref[...] = _dot(a_ref[...], b_ref[...]).astype(o_ref.dtype)


def _matmul(a, w, out_dtype, name):
    M, K = a.shape
    N = w.shape[1]
    tm = _pick(M, (1024, 512, 256, 128))
    tn = _pick(N, (512, 384, 256, 128))
    return pl.pallas_call(
        _mm_kernel,
        grid=(M // tm, N // tn),
        in_specs=[
            pl.BlockSpec((tm, K), lambda i, j: (i, 0)),
            pl.BlockSpec((K, tn), lambda i, j: (0, j)),
        ],
        out_specs=pl.BlockSpec((tm, tn), lambda i, j: (i, j)),
        out_shape=jax.ShapeDtypeStruct((M, N), out_dtype),
        compiler_params=_params(("arbitrary", "arbitrary")),
        name=name,
    )(a, w)


def _cum_kernel(f_ref, b_ref, o_ref, carry_ref):
    @pl.when(pl.program_id(1) == 0)
    def _():
        carry_ref[...] = jnp.zeros_like(carry_ref)

    ts = f_ref.shape[0]
    lf = _log_sigmoid(f_ref[...] + b_ref[...])
    p1 = lf.astype(BF16)
    r1 = lf - p1.astype(F32)
    p2 = r1.astype(BF16)
    p3 = (r1 - p2.astype(F32)).astype(BF16)
    row = lax.broadcasted_iota(jnp.int32, (ts, ts), 0)
    col = lax.broadcasted_iota(jnp.int32, (ts, ts), 1)
    tri = jnp.where(col <= row, 1.0, 0.0).astype(BF16)
    cum = _dot(tri, p1) + _dot(tri, p2) + _dot(tri, p3) + carry_ref[...]
    o_ref[...] = cum
    carry_ref[...] = cum[ts - 1:ts, :]


def _forget_cumsum(rest, b_pad, batch, seq, col_block):
    T = rest.shape[0]
    ts = _pick(seq, (256, 128))
    ns = seq // ts
    return pl.pallas_call(
        _cum_kernel,
        grid=(batch, ns),
        in_specs=[
            pl.BlockSpec((ts, LANES), lambda b, s: (b * ns + s, col_block)),
            pl.BlockSpec((1, LANES), lambda b, s: (0, 0)),
        ],
        out_specs=pl.BlockSpec((ts, LANES), lambda b, s: (b * ns + s, 0)),
        out_shape=jax.ShapeDtypeStruct((T, LANES), F32),
        scratch_shapes=[pltpu.VMEM((1, LANES), F32)],
        compiler_params=_params(("arbitrary", "arbitrary")),
        name="forget_cumsum",
    )(rest, b_pad)


def _masked_heads(q_ref, pairs):
    lane = lax.broadcasted_iota(jnp.int32, (1, LANES), 1)
    heads = []
    for p in range(pairs):
        q2 = q_ref[:, p * LANES:(p + 1) * LANES] * jnp.asarray(HEAD_DIM ** -0.5, q_ref.dtype)
        zero = jnp.zeros_like(q2)
        heads += [jnp.where(lane < HEAD_DIM, q2, zero), jnp.where(lane >= HEAD_DIM, q2, zero)]
    return heads


def _pairs_per_step(cands, *counts):
    return next(c for c in cands if all(n % c == 0 for n in counts))


def _store_value_transposed(v_ref, vt_ref, first):
    pairs, nk, _, two_tk = vt_ref.shape
    tk = two_tk // 2
    for p in range(pairs):
        for j in range(nk):
            vt = v_ref[j * tk:(j + 1) * tk, p * LANES:(p + 1) * LANES].astype(F32).T
            both = jnp.concatenate([jnp.where(first, vt, 0.0), jnp.where(first, 0.0, vt)], axis=1)
            vt_ref[p, j] = both.astype(BF16)


def _sb_kernel(q_ref, k_ref, v_ref, o_ref, vt_ref, *, pairs):
    tq = q_ref.shape[0]
    i = pl.program_id(2)
    q_heads = _masked_heads(q_ref, pairs)
    first = lax.broadcasted_iota(jnp.int32, (LANES, 1), 0) < HEAD_DIM

    @pl.when(i == 0)
    def _():
        for p in range(pairs):
            for j in range(vt_ref.shape[1]):
                vt = v_ref[j * tq:(j + 1) * tq, p * LANES:(p + 1) * LANES].astype(F32).T
                vt_ref[p, j] = vt.astype(BF16)

    q_pair = [jnp.concatenate([q_heads[2 * p], q_heads[2 * p + 1]], axis=0) for p in range(pairs)]
    key = lax.broadcasted_iota(jnp.int32, (tq, tq), 0)
    qry = lax.broadcasted_iota(jnp.int32, (tq, tq), 1)
    later_keys = jnp.where(qry > key, 1.0, 0.0).astype(BF16)
    causal = jnp.concatenate([key < qry, key < qry], axis=1)

    def block(j, state, diagonal):
        start = pl.multiple_of(j * tq, tq)
        out = []
        for p in range(pairs):
            run, acc = state[2 * p], state[2 * p + 1]
            z = _dot_nt(k_ref[pl.ds(start, tq), p * LANES:(p + 1) * LANES], q_pair[p])
            ls = jnp.minimum(z, 0.0) - jnp.log(1.0 + jnp.exp(-jnp.abs(z)))
            lk = ls - z
            if diagonal:
                lk = jnp.where(causal, lk, 0.0)
            lk_hi, lk_lo = _split2(lk)
            later = _dot(later_keys, lk_hi) + _dot(later_keys, lk_lo)
            w = jnp.exp(ls + later + run)
            if diagonal:
                w = jnp.where(causal, w, 0.0)
            pv = _dot(vt_ref[p, j], w.astype(BF16))
            acc = acc + jnp.where(first, pv[:, :tq], pv[:, tq:])
            out += [run + jnp.sum(lk, axis=0, keepdims=True), acc]
        return tuple(out)

    def live(state):
        top = state[0]
        for p in range(1, pairs):
            top = jnp.maximum(top, state[2 * p])
        return (jnp.max(top) > SB_DEAD_LOG).astype(jnp.int32)

    init = (jnp.zeros((1, 2 * tq), F32), jnp.zeros((LANES, tq), F32)) * pairs
    state = block(i, init, True)

    def cond(carry):
        return (carry[0] >= 0) & (carry[1] > 0)

    def body(carry):
        state = block(carry[0], carry[2:], False)
        return (carry[0] - 1, live(state)) + state

    res = lax.while_loop(cond, body, (i - 1, live(state)) + state)
    slabs = [res[2 + 2 * p + 1].T for p in range(pairs)]
    o = slabs[0] if pairs == 1 else jnp.concatenate(slabs, axis=1)
    o_ref[...] = o.astype(o_ref.dtype)


def _sb_attention(qkv, batch, seq, n_pairs, q_col, k_col, v_col):
    T = qkv.shape[0]
    tq = _pick(seq, (256, 128))
    nq = seq // tq
    pairs = _pairs_per_step((4, 2, 1), n_pairs, q_col, k_col, v_col)
    W = pairs * LANES
    qc, kc, vc = q_col // pairs, k_col // pairs, v_col // pairs
    return pl.pallas_call(
        functools.partial(_sb_kernel, pairs=pairs),
        grid=(batch, n_pairs // pairs, nq),
        in_specs=[
            pl.BlockSpec((tq, W), lambda b, p, i: (b * nq + i, qc + p)),
            pl.BlockSpec((seq, W), lambda b, p, i: (b, kc + p)),
            pl.BlockSpec((seq, W), lambda b, p, i: (b, vc + p)),
        ],
        out_specs=pl.BlockSpec((tq, W), lambda b, p, i: (b * nq + i, p)),
        out_shape=jax.ShapeDtypeStruct((T, n_pairs * LANES), BF16),
        scratch_shapes=[pltpu.VMEM((pairs, nq, LANES, tq), BF16)],
        compiler_params=_params(("arbitrary", "arbitrary", "arbitrary")),
        name="sb_attention",
    )(qkv, qkv, qkv)


def _fox_kernel(q_ref, k_ref, v_ref, cc_ref, o_ref, vt_ref, *, pairs):
    tq = q_ref.shape[0]
    i = pl.program_id(2)
    q_heads = _masked_heads(q_ref, pairs)
    first = lax.broadcasted_iota(jnp.int32, (LANES, 1), 0) < HEAD_DIM

    @pl.when(i == 0)
    def _():
        _store_value_transposed(v_ref, vt_ref, first)

    key = lax.broadcasted_iota(jnp.int32, (tq, tq), 0)
    qry = lax.broadcasted_iota(jnp.int32, (tq, tq), 1)

    def block(j, carry, diagonal):
        start = pl.multiple_of(j * tq, tq)
        out = []
        for p in range(pairs):
            kblk = k_ref[pl.ds(start, tq), p * LANES:(p + 1) * LANES]
            acc = carry[5 * p + 4]
            stats = []
            for s in range(2):
                m, l = carry[5 * p + 2 * s], carry[5 * p + 2 * s + 1]
                logits = _dot_nt(kblk, q_heads[2 * p + s]) - cc_ref[p, pl.ds(start, tq), s:s + 1]
                if diagonal:
                    logits = jnp.where(key <= qry, logits, NEG_BIG)
                m_new = jnp.maximum(m, jnp.max(logits, axis=0, keepdims=True))
                alpha = jnp.exp(m - m_new)
                prob = jnp.exp(logits - m_new)
                l = l * alpha + jnp.sum(prob, axis=0, keepdims=True)
                stats.append((m_new, l, alpha, prob.astype(BF16)))
            pv = _dot(vt_ref[p, j], jnp.concatenate([stats[0][3], stats[1][3]], axis=0))
            acc = acc * jnp.where(first, stats[0][2], stats[1][2]) + pv
            out += [stats[0][0], stats[0][1], stats[1][0], stats[1][1], acc]
        return tuple(out)

    row_stat = (jnp.full((1, tq), NEG_BIG, F32), jnp.zeros((1, tq), F32))
    init = (row_stat * 2 + (jnp.zeros((LANES, tq), F32),)) * pairs
    state = block(i, init, True)
    res = lax.fori_loop(0, i, lambda j, c: block(j, c, False), state)
    slabs = []
    for p in range(pairs):
        inv = jnp.where(first, 1.0 / res[5 * p + 1], 1.0 / res[5 * p + 3])
        slabs.append((res[5 * p + 4] * inv).T)
    o = slabs[0] if pairs == 1 else jnp.concatenate(slabs, axis=1)
    o_ref[...] = o.astype(o_ref.dtype)


def _fox_attention(qkv, cum_cols, batch, seq, tq, n_pairs, q_col, k_col, v_col):
    T = qkv.shape[0]
    nq = seq // tq
    pairs = _pairs_per_step((FOX_PAIRS, 1), n_pairs, q_col, k_col, v_col)
    W = pairs * LANES
    qc, kc, vc = q_col // pairs, k_col // pairs, v_col // pairs
    return pl.pallas_call(
        functools.partial(_fox_kernel, pairs=pairs),
        grid=(batch, n_pairs // pairs, nq),
        in_specs=[
            pl.BlockSpec((tq, W), lambda b, p, i: (b * nq + i, qc + p)),
            pl.BlockSpec((seq, W), lambda b, p, i: (b, kc + p)),
            pl.BlockSpec((seq, W), lambda b, p, i: (b, vc + p)),
            pl.BlockSpec((pairs, seq, 2), lambda b, p, i: (p, b, 0)),
        ],
        out_specs=pl.BlockSpec((tq, W), lambda b, p, i: (b * nq + i, p)),
        out_shape=jax.ShapeDtypeStruct((T, n_pairs * LANES), BF16),
        scratch_shapes=[pltpu.VMEM((pairs, nq, LANES, 2 * tq), BF16)],
        compiler_params=_params(("arbitrary", "arbitrary", "arbitrary")),
        name="fox_attention",
    )(qkv, qkv, qkv, cum_cols)


def _conv_kernel(a_ref, g_ref, w_ref, cb_ref, lg_ref, lb_ref, o_ref, u_ref, s_ref, *, width, halo):
    ts = a_ref.shape[0]

    @pl.when(pl.program_id(1) == 0)
    def _():
        u_ref[0:halo, :] = jnp.zeros((halo, u_ref.shape[1]), F32)

    u_ref[halo:halo + ts, :] = a_ref[...] * _sigmoid(g_ref[...])
    span = ts + halo - SUBLANES
    for r in range(1, SUBLANES):
        s_ref[r - 1, 0:span, :] = u_ref[r:r + span, :]
    acc = jnp.zeros(a_ref.shape, F32) + cb_ref[...]
    for k in range(width):
        base, r = divmod(halo - (width - 1) + k, SUBLANES)
        lo = base * SUBLANES
        tap = u_ref[lo:lo + ts, :] if r == 0 else s_ref[r - 1, lo:lo + ts, :]
        acc = acc + w_ref[k:k + 1, :] * tap
    y = _ln_rows(acc) * lg_ref[...] + lb_ref[...]
    o_ref[...] = (y * _sigmoid(y)).astype(o_ref.dtype)
    u_ref[0:halo, :] = u_ref[ts:ts + halo, :]


def _conv_module(rest, conv_w, conv_b, ln_g, ln_b, batch, seq):
    T = rest.shape[0]
    width, C = conv_w.shape
    halo = -(-(width - 1) // SUBLANES) * SUBLANES
    ts = _pick(seq, (128,))
    ns = seq // ts
    kern = functools.partial(_conv_kernel, width=width, halo=halo)
    vec = lambda: pl.BlockSpec((1, C), lambda b, s: (0, 0))
    return pl.pallas_call(
        kern,
        grid=(batch, ns),
        in_specs=[
            pl.BlockSpec((ts, C), lambda b, s: (b * ns + s, 0)),
            pl.BlockSpec((ts, C), lambda b, s: (b * ns + s, 1)),
            pl.BlockSpec((width, C), lambda b, s: (0, 0)),
            vec(), vec(), vec(),
        ],
        out_specs=pl.BlockSpec((ts, C), lambda b, s: (b * ns + s, 0)),
        out_shape=jax.ShapeDtypeStruct((T, C), BF16),
        scratch_shapes=[pltpu.VMEM((ts + halo, C), F32),
                        pltpu.VMEM((SUBLANES - 1, ts + halo, C), F32)],
        compiler_params=_params(("arbitrary", "arbitrary")),
        name="conformer_conv",
    )(rest, rest, conv_w, conv_b.reshape(1, C), ln_g.reshape(1, C), ln_b.reshape(1, C))


def _outproj_kernel(osb_ref, ofx_ref, ocv_ref, w_ref, x_ref, gt_ref, g_ref, b_ref,
                    sc_ref, sh_ref, rwh_ref, rwl_ref, rb_ref,
                    x1_ref, h2_ref, lg_ref, *, alpha):
    d_sb = osb_ref.shape[1]
    d_fx = ofx_ref.shape[1]
    y = _dot(osb_ref[...], w_ref[0:d_sb, :])
    y = y + _dot(ofx_ref[...], w_ref[d_sb:d_sb + d_fx, :])
    y = y + _dot(ocv_ref[...], w_ref[d_sb + d_fx:, :])
    x1 = _ln_rows(alpha * x_ref[...] + gt_ref[0] * y) * g_ref[...] + b_ref[...]
    x1_ref[...] = x1
    h2 = _ln_rows(x1) * (1.0 + sc_ref[0]) + sh_ref[0]
    h2_ref[...] = h2
    h_hi, h_lo = _split2(h2)
    lg = _dot(h_hi, rwh_ref[...]) + _dot(h_lo, rwh_ref[...]) + _dot(h_hi, rwl_ref[...])
    lg_ref[...] = lg + rb_ref[...]


def _outproj(o_sb, o_fx, o_cv, w_out, x2d, gt, g, b, sc2, sh2, rw_hi, rw_lo, rb, seq, alpha):
    T, D = x2d.shape
    tm = _pick(seq, (256, 128))
    per_b = seq // tm
    rowblk = lambda n: pl.BlockSpec((tm, n), lambda i: (i, 0))
    full = lambda r, c: pl.BlockSpec((r, c), lambda i: (0, 0))
    perb = lambda: pl.BlockSpec((1, 1, D), lambda i: (i // per_b, 0, 0))
    return pl.pallas_call(
        functools.partial(_outproj_kernel, alpha=alpha),
        grid=(T // tm,),
        in_specs=[
            rowblk(o_sb.shape[1]), rowblk(o_fx.shape[1]), rowblk(o_cv.shape[1]),
            full(D, D), rowblk(D), perb(), full(1, D), full(1, D), perb(), perb(),
            full(D, LANES), full(D, LANES), full(1, LANES),
        ],
        out_specs=[rowblk(D), rowblk(D), rowblk(LANES)],
        out_shape=[jax.ShapeDtypeStruct((T, D), F32), jax.ShapeDtypeStruct((T, D), F32),
                   jax.ShapeDtypeStruct((T, LANES), F32)],
        compiler_params=_params(("arbitrary",)),
        name="out_proj_norm_router",
    )(o_sb, o_fx, o_cv, w_out, x2d, gt, g, b, sc2, sh2, rw_hi, rw_lo, rb)


def _route_kernel(lg_ref, e_ref, w_ref, *, n_groups, epg):
    L = lg_ref[...]
    lane_i = lax.broadcasted_iota(jnp.int32, L.shape, 1)
    lane = lane_i.astype(F32)
    far = 1e6
    is_grp = lane < n_groups
    l1 = jnp.where(is_grp, L, NEG_BIG)
    m1 = jnp.max(l1, axis=1, keepdims=True)
    grp = jnp.min(jnp.where(is_grp & (l1 == m1), lane, far), axis=1, keepdims=True)
    s1 = jnp.sum(jnp.where(is_grp, jnp.exp(l1 - m1), 0.0), axis=1, keepdims=True)
    p_grp = 1.0 / s1
    lo = n_groups + grp * epg
    in_grp = (lane >= lo) & (lane < lo + epg)
    l2 = jnp.where(in_grp, L, NEG_BIG)
    m2 = jnp.max(l2, axis=1, keepdims=True)
    e2 = jnp.where(in_grp, jnp.exp(l2 - m2), -1.0)
    v1 = jnp.max(e2, axis=1, keepdims=True)
    i1 = jnp.min(jnp.where(e2 == v1, lane, far), axis=1, keepdims=True)
    e2b = jnp.where(lane == i1, -1.0, e2)
    v2 = jnp.max(e2b, axis=1, keepdims=True)
    i2 = jnp.min(jnp.where(e2b == v2, lane, far), axis=1, keepdims=True)
    den = v1 + v2
    w0 = p_grp * v1 / den
    w1 = p_grp * v2 / den
    ids = jnp.where(lane_i == 0, i1 - n_groups, jnp.where(lane_i == 1, i2 - n_groups, 0.0))
    e_ref[...] = ids.astype(jnp.int32)
    w_ref[...] = jnp.where(lane_i == 0, w0, jnp.where(lane_i == 1, w1, 0.0))


def _route(logits, n_groups, epg):
    T = logits.shape[0]
    tm = _pick(T, (512, 256, 128))
    blk = lambda: pl.BlockSpec((tm, LANES), lambda i: (i, 0))
    return pl.pallas_call(
        functools.partial(_route_kernel, n_groups=n_groups, epg=epg),
        grid=(T // tm,),
        in_specs=[blk()],
        out_specs=[blk(), blk()],
        out_shape=[jax.ShapeDtypeStruct((T, LANES), jnp.int32),
                   jax.ShapeDtypeStruct((T, LANES), F32)],
        compiler_params=_params(("arbitrary",)),
        name="route_topk",
    )(logits)


def _plan_kernel(e_ref, dest_ref, be_ref, cnt_ref, base_ref, *, n_experts):
    p = pl.program_id(0)
    i = pl.program_id(1)
    tm = e_ref.shape[0]
    lane = lax.broadcasted_iota(jnp.int32, (tm, LANES), 1)
    e = e_ref[...]
    hot0 = lane == e[:, 0:1]
    hot1 = lane == e[:, 1:2]
    both = jnp.where(hot0 | hot1, 1.0, 0.0)

    @pl.when((p == 0) & (i == 0))
    def _():
        cnt_ref[...] = jnp.zeros_like(cnt_ref)

    @pl.when(p == 0)
    def _():
        cnt_ref[...] += jnp.sum(both, axis=0, keepdims=True)

    @pl.when((p == 1) & (i == 0))
    def _():
        nblk = jnp.floor((cnt_ref[...] + (MOE_BLOCK - 1)) * (1.0 / MOE_BLOCK))
        r = lax.broadcasted_iota(jnp.int32, (LANES, LANES), 0)
        c = lax.broadcasted_iota(jnp.int32, (LANES, LANES), 1)
        before = jnp.where(r < c, 1.0, 0.0).astype(BF16)
        nb8 = jnp.broadcast_to(nblk, (SUBLANES, LANES)).astype(BF16)
        excl = _dot(nb8, before)[0:1, :]
        base_ref[...] = excl * MOE_BLOCK
        cnt_ref[...] = jnp.zeros_like(cnt_ref)
        incl = excl + nblk
        nb_rows = be_ref.shape[0]
        bidx = lax.broadcasted_iota(jnp.int32, (nb_rows, LANES), 0).astype(F32)
        lane_b = lax.broadcasted_iota(jnp.int32, (nb_rows, LANES), 1)
        done = jnp.where((incl <= bidx) & (lane_b < n_experts), 1.0, 0.0)
        be = jnp.sum(done, axis=1, keepdims=True)
        be_ref[...] = jnp.broadcast_to(be, (nb_rows, LANES)).astype(jnp.int32)

    @pl.when(p == 1)
    def _():
        r = lax.broadcasted_iota(jnp.int32, (tm, tm), 0)
        c = lax.broadcasted_iota(jnp.int32, (tm, tm), 1)
        earlier = jnp.where(c < r, 1.0, 0.0).astype(BF16)
        tot = _dot(earlier, both.astype(BF16)) + cnt_ref[...] + base_ref[...]
        d0 = jnp.sum(jnp.where(hot0, tot, 0.0), axis=1, keepdims=True)
        d1 = jnp.sum(jnp.where(hot1, tot, 0.0), axis=1, keepdims=True)
        dest = jnp.where(lane == 0, d0, jnp.where(lane == 1, d1, 0.0))
        dest_ref[...] = dest.astype(jnp.int32)
        cnt_ref[...] += jnp.sum(both, axis=0, keepdims=True)


def _plan(e_lanes, n_experts, n_blocks):
    T = e_lanes.shape[0]
    tm = _pick(T, (256, 128))
    nb_rows = -(-n_blocks // SUBLANES) * SUBLANES
    return pl.pallas_call(
        functools.partial(_plan_kernel, n_experts=n_experts),
        grid=(2, T // tm),
        in_specs=[pl.BlockSpec((tm, LANES), lambda p, i: (i, 0))],
        out_specs=[pl.BlockSpec((tm, LANES), lambda p, i: (i * p, 0)),
                   pl.BlockSpec((nb_rows, LANES), lambda p, i: (0, 0))],
        out_shape=[jax.ShapeDtypeStruct((T, LANES), jnp.int32),
                   jax.ShapeDtypeStruct((nb_rows, LANES), jnp.int32)],
        scratch_shapes=[pltpu.VMEM((1, LANES), F32), pltpu.VMEM((1, LANES), F32)],
        compiler_params=_params(("arbitrary", "arbitrary")),
        name="dispatch_plan",
    )(e_lanes)


def _row_copy(src, src_row, dst, dst_row, sem):
    return pltpu.make_async_copy(src.at[pl.ds(src_row, 1)], dst.at[pl.ds(dst_row, 1)], sem)


def _slotmap_kernel(dest_ref, tok_ref):
    def clear(s, carry):
        tok_ref[s] = 0
        return carry

    lax.fori_loop(0, tok_ref.shape[0], clear, 0, unroll=8)

    def put(t, carry):
        tok_ref[dest_ref[2 * t]] = t
        tok_ref[dest_ref[2 * t + 1]] = t
        return carry

    lax.fori_loop(0, dest_ref.shape[0] // 2, put, 0, unroll=8)


def _slot_map(dest_flat, n_slots):
    return pl.pallas_call(
        _slotmap_kernel,
        in_specs=[pl.BlockSpec(memory_space=pltpu.SMEM)],
        out_specs=pl.BlockSpec(memory_space=pltpu.SMEM),
        out_shape=jax.ShapeDtypeStruct((n_slots,), jnp.int32),
        name="moe_slot_map",
    )(dest_flat)


def _expert_kernel(be_ref, nu_ref, tok_ref, h_ref, wg_ref, wu_ref, wd_ref, o_ref, xbuf, sems):
    del be_ref
    b = pl.program_id(0)
    n_used = nu_ref[0]
    slot = b % 2

    def issue(blk, s, r):
        _row_copy(h_ref, tok_ref[blk * MOE_BLOCK + r], xbuf.at[s], r, sems.at[s]).start()

    @pl.when(b == 0)
    def _():
        def step(r, carry):
            issue(0, 0, r)
            return carry

        lax.fori_loop(0, MOE_BLOCK, step, 0, unroll=8)

    @pl.when(b + 1 < n_used)
    def _():
        for r in range(MOE_BLOCK):
            issue(b + 1, 1 - slot, r)

    @pl.when(b < n_used)
    def _():
        def drain(r, carry):
            _row_copy(h_ref, 0, xbuf.at[slot], 0, sems.at[slot]).wait()
            return carry

        lax.fori_loop(0, MOE_BLOCK, drain, 0, unroll=8)
        x = xbuf[slot].astype(BF16)
        g = _dot(x, wg_ref[0, 0])
        u = _dot(x, wu_ref[0, 0])
        hid = (g * _sigmoid(g)) * u
        o_ref[...] = _dot(hid.astype(BF16), wd_ref[0, 0])

    @pl.when(b >= n_used)
    def _():
        o_ref[...] = jnp.zeros_like(o_ref)


def _experts(block_expert, n_used, slot_tok, h2, w_gate, w_up, w_down, layer, n_blocks):
    D = h2.shape[1]
    DE = w_gate.shape[3]
    wmap = lambda b, be, nu, tok: (layer, be[jnp.minimum(b, nu[0] - 1)], 0, 0)
    return pl.pallas_call(
        _expert_kernel,
        grid_spec=pltpu.PrefetchScalarGridSpec(
            num_scalar_prefetch=3,
            grid=(n_blocks,),
            in_specs=[
                pl.BlockSpec(memory_space=pl.ANY),
                pl.BlockSpec((1, 1, D, DE), wmap),
                pl.BlockSpec((1, 1, D, DE), wmap),
                pl.BlockSpec((1, 1, DE, D), wmap),
            ],
            out_specs=pl.BlockSpec((MOE_BLOCK, D), lambda b, be, nu, tok: (b, 0)),
            scratch_shapes=[pltpu.VMEM((2, MOE_BLOCK, D), F32), pltpu.SemaphoreType.DMA((2,))],
        ),
        out_shape=jax.ShapeDtypeStruct((n_blocks * MOE_BLOCK, D), F32),
        compiler_params=_params(("arbitrary",)),
        name="moe_experts",
    )(block_expert, n_used, slot_tok, h2, w_gate, w_up, w_down)


def _combine_kernel(dest_ref, yb_ref, w_ref, x_ref, gt_ref, g_ref, b_ref, o_ref,
                    buf_ref, sems, *, alpha):
    tm = x_ref.shape[0]
    i = pl.program_id(0)
    slot = i % 2

    def issue(tile, s, r):
        t = tile * tm + r
        _row_copy(yb_ref, dest_ref[2 * t], buf_ref.at[s, 0], r, sems.at[s]).start()
        _row_copy(yb_ref, dest_ref[2 * t + 1], buf_ref.at[s, 1], r, sems.at[s]).start()

    @pl.when(i == 0)
    def _():
        def step(r, carry):
            issue(0, 0, r)
            return carry

        lax.fori_loop(0, tm, step, 0, unroll=8)

    @pl.when(i + 1 < pl.num_programs(0))
    def _():
        for r in range(tm):
            issue(i + 1, 1 - slot, r)

    def drain(r, carry):
        _row_copy(yb_ref, 0, buf_ref.at[slot, 0], 0, sems.at[slot]).wait()
        return carry

    lax.fori_loop(0, 2 * tm, drain, 0, unroll=8)
    w = w_ref[...]
    y = w[:, 0:1] * buf_ref[slot, 0] + w[:, 1:2] * buf_ref[slot, 1]
    v = alpha * x_ref[...] + gt_ref[0] * y
    o_ref[...] = _ln_rows(v) * g_ref[...] + b_ref[...]


def _combine(dest_flat, yb, w_lanes, x1, gt, g, b, seq, alpha):
    T, D = x1.shape
    tm = _pick(seq, (256, 128))
    per_b = seq // tm
    return pl.pallas_call(
        functools.partial(_combine_kernel, alpha=alpha),
        grid_spec=pltpu.PrefetchScalarGridSpec(
            num_scalar_prefetch=1,
            grid=(T // tm,),
            in_specs=[
                pl.BlockSpec(memory_space=pl.ANY),
                pl.BlockSpec((tm, LANES), lambda i, d: (i, 0)),
                pl.BlockSpec((tm, D), lambda i, d: (i, 0)),
                pl.BlockSpec((1, 1, D), lambda i, d: (i // per_b, 0, 0)),
                pl.BlockSpec((1, D), lambda i, d: (0, 0)),
                pl.BlockSpec((1, D), lambda i, d: (0, 0)),
            ],
            out_specs=pl.BlockSpec((tm, D), lambda i, d: (i, 0)),
            scratch_shapes=[pltpu.VMEM((2, 2, tm, D), F32), pltpu.SemaphoreType.DMA((2,))],
        ),
        out_shape=jax.ShapeDtypeStruct((T, D), F32),
        compiler_params=_params(("arbitrary",)),
        name="moe_combine_norm",
    )(dest_flat, yb, w_lanes, x1, gt, g, b)


def kernel(x, c, ada_w, ada_b, w_in, b_forget, conv_w, conv_b, conv_ln_g, conv_ln_b, w_out,
           ln1_g, ln1_b, r1_w, r1_b, r2_w, r2_b, w_gate, w_up, w_down, ln2_g, ln2_b):
    B, S, D = x.shape
    L = ada_w.shape[0]
    T = B * S
    alpha = float((2 * L) ** 0.25)
    d_sb, d_fx, c_cv = D // 4, D // 2, D // 4
    n_fx = d_fx // HEAD_DIM
    n_groups = r1_w.shape[-1]
    epg = r2_w.shape[-1]
    n_experts = n_groups * epg
    n_blocks = (2 * T) // MOE_BLOCK + n_experts
    n_slots = n_blocks * MOE_BLOCK
    qkv_cols = 3 * d_sb + 3 * d_fx
    tq = _pick(S, (512, 256, 128))

    rows = -(-B // SUBLANES) * SUBLANES
    c_pad = jnp.zeros((rows, D), F32).at[:B].set(c)
    mod_all = _ada(c_pad, ada_w, ada_b)

    wg16, wu16, wd16 = w_gate.astype(BF16), w_up.astype(BF16), w_down.astype(BF16)
    x2d = x.reshape(T, D)
    for l in range(L):
        mod = mod_all[l, :B]
        sh1, sc1, gt1, sh2, sc2, gt2 = [m.reshape(B, 1, D) for m in jnp.split(mod, 6, axis=-1)]

        h1 = _lnmod(x2d, sc1, sh1, S)
        w_qkv = w_in[l, :, :qkv_cols].astype(BF16)
        f_lo = qkv_cols
        g_lo = qkv_cols + n_fx
        w_rest = jnp.concatenate(
            [w_in[l, :, g_lo:g_lo + 2 * c_cv], w_in[l, :, f_lo:f_lo + n_fx],
             jnp.zeros((D, LANES - n_fx), F32)], axis=1).astype(BF16)
        qkv = _matmul(h1, w_qkv, BF16, "proj_qkv")
        rest = _matmul(h1, w_rest, F32, "proj_glu_forget")

        b_pad = jnp.zeros((1, LANES), F32).at[0, :n_fx].set(b_forget[l])
        cum = _forget_cumsum(rest, b_pad, B, S, (2 * c_cv) // LANES)[:, :n_fx]
        cum_cols = cum.reshape(T, n_fx // 2, 2).transpose(1, 0, 2)

        nb = LANES
        o_sb = _sb_attention(qkv, B, S, d_sb // nb, 0, d_sb // nb, 2 * d_sb // nb)
        fx0 = 3 * d_sb // nb
        o_fx = _fox_attention(qkv, cum_cols, B, S, tq, d_fx // nb, fx0, fx0 + d_fx // nb,
                              fx0 + 2 * d_fx // nb)
        o_cv = _conv_module(rest, conv_w[l], conv_b[l], conv_ln_g[l], conv_ln_b[l], B, S)

        rw = jnp.concatenate(
            [r1_w[l], r2_w[l].transpose(1, 0, 2).reshape(D, n_experts),
             jnp.zeros((D, LANES - n_groups - n_experts), F32)], axis=1)
        rw_hi = rw.astype(BF16)
        rw_lo = (rw - rw_hi.astype(F32)).astype(BF16)
        rb = jnp.zeros((1, LANES), F32).at[0, :n_groups].set(r1_b[l])
        rb = rb.at[0, n_groups:n_groups + n_experts].set(r2_b[l].reshape(-1))
        x1, h2, logits = _outproj(
            o_sb, o_fx, o_cv, w_out[l].astype(BF16), x2d, gt1,
            ln1_g[l].reshape(1, D), ln1_b[l].reshape(1, D), sc2, sh2, rw_hi, rw_lo, rb, S, alpha)

        e_lanes, w_lanes = _route(logits, n_groups, epg)
        dest_lanes, be_lanes = _plan(e_lanes, n_experts, n_blocks)
        dest_flat = dest_lanes[:, :2].reshape(-1)
        be = be_lanes[:n_blocks, 0]
        n_used = jnp.sum((be < n_experts).astype(jnp.int32)).reshape(1)
        block_expert = jnp.minimum(be, n_experts - 1)
        slot_tok = _slot_map(dest_flat, n_slots)
        yb = _experts(block_expert, n_used, slot_tok, h2, wg16, wu16, wd16, l, n_blocks)
        x2d = _combine(dest_flat, yb, w_lanes, x1, gt2,
                       ln2_g[l].reshape(1, D), ln2_b[l].reshape(1, D), S, alpha)
    return x2d.reshape(B, S, D)
```

```python
import functools

import jax
import jax.numpy as jnp
from jax import lax
from jax.experimental import pallas as pl
from jax.experimental.pallas import tpu as pltpu

LN_EPS = 1e-5
HEAD_DIM = 64
LANES = 128
SUBLANES = 8
MOE_BLOCK = 256
NEG_BIG = -1e30
SB_DEAD_LOG = -104.0
FOX_PAIRS = 2
VMEM_LIMIT = 56 * 1024 * 1024

F32 = jnp.float32
BF16 = jnp.bfloat16


def _pick(n, cands):
    for c in cands:
        if n % c == 0:
            return c
    return n


def _params(sem):
    return pltpu.CompilerParams(dimension_semantics=sem, vmem_limit_bytes=VMEM_LIMIT)


def _ln_rows(v):
    mu = jnp.mean(v, axis=-1, keepdims=True)
    d = v - mu
    var = jnp.mean(d * d, axis=-1, keepdims=True)
    return d * lax.rsqrt(var + LN_EPS)


def _log_sigmoid(z):
    return jnp.minimum(z, 0.0) - jnp.log1p(jnp.exp(-jnp.abs(z)))


def _sigmoid(z):
    return 1.0 / (1.0 + jnp.exp(-z))


def _split2(v):
    hi = v.astype(BF16)
    lo = (v - hi.astype(F32)).astype(BF16)
    return hi, lo


def _dot(a, b):
    return jnp.dot(a, b, preferred_element_type=F32)


def _dot_nt(a, b):
    return lax.dot_general(a, b, (((1,), (1,)), ((), ())), preferred_element_type=F32)


def _ada_kernel(c_ref, w_ref, b_ref, o_ref):
    c = c_ref[...]
    s = c * _sigmoid(c)
    s_hi, s_lo = _split2(s)
    w_hi, w_lo = _split2(w_ref[0])
    acc = _dot(s_hi, w_hi) + _dot(s_lo, w_hi) + _dot(s_hi, w_lo)
    o_ref[0] = acc + b_ref[0]


def _ada(c_pad, ada_w, ada_b):
    L, D, N = ada_w.shape
    rows = c_pad.shape[0]
    tn = _pick(N, (512, 256, 128))
    return pl.pallas_call(
        _ada_kernel,
        grid=(L, N // tn),
        in_specs=[
            pl.BlockSpec((rows, D), lambda l, n: (0, 0)),
            pl.BlockSpec((1, D, tn), lambda l, n: (l, 0, n)),
            pl.BlockSpec((1, 1, tn), lambda l, n: (l, 0, n)),
        ],
        out_specs=pl.BlockSpec((1, rows, tn), lambda l, n: (l, 0, n)),
        out_shape=jax.ShapeDtypeStruct((L, rows, N), F32),
        compiler_params=_params(("arbitrary", "arbitrary")),
        name="ada_mod",
    )(c_pad, ada_w, ada_b.reshape(L, 1, N))


def _lnmod_kernel(x_ref, sc_ref, sh_ref, o_ref):
    h = _ln_rows(x_ref[...]) * (1.0 + sc_ref[0]) + sh_ref[0]
    o_ref[...] = h.astype(o_ref.dtype)


def _lnmod(x2d, sc, sh, seq):
    T, D = x2d.shape
    tm = _pick(seq, (512, 256, 128))
    per_b = seq // tm
    return pl.pallas_call(
        _lnmod_kernel,
        grid=(T // tm,),
        in_specs=[
            pl.BlockSpec((tm, D), lambda i: (i, 0)),
            pl.BlockSpec((1, 1, D), lambda i: (i // per_b, 0, 0)),
            pl.BlockSpec((1, 1, D), lambda i: (i // per_b, 0, 0)),
        ],
        out_specs=pl.BlockSpec((tm, D), lambda i: (i, 0)),
        out_shape=jax.ShapeDtypeStruct((T, D), BF16),
        compiler_params=_params(("arbitrary",)),
        name="ln_mod",
    )(x2d, sc, sh)


def _mm_kernel(a_ref, b_ref, o_ref):
    o_ref[...] = _dot(a_ref[...], b_ref[...]).astype(o_ref.dtype)


def _matmul(a, w, out_dtype, name):
    M, K = a.shape
    N = w.shape[1]
    tm = _pick(M, (1024, 512, 256, 128))
    tn = _pick(N, (512, 384, 256, 128))
    return pl.pallas_call(
        _mm_kernel,
        grid=(M // tm, N // tn),
        in_specs=[
            pl.BlockSpec((tm, K), lambda i, j: (i, 0)),
            pl.BlockSpec((K, tn), lambda i, j: (0, j)),
        ],
        out_specs=pl.BlockSpec((tm, tn), lambda i, j: (i, j)),
        out_shape=jax.ShapeDtypeStruct((M, N), out_dtype),
        compiler_params=_params(("arbitrary", "arbitrary")),
        name=name,
    )(a, w)


def _cum_kernel(f_ref, b_ref, o_ref, carry_ref):
    @pl.when(pl.program_id(1) == 0)
    def _():
        carry_ref[...] = jnp.zeros_like(carry_ref)

    ts = f_ref.shape[0]
    lf = _log_sigmoid(f_ref[...] + b_ref[...])
    p1 = lf.astype(BF16)
    r1 = lf - p1.astype(F32)
    p2 = r1.astype(BF16)
    p3 = (r1 - p2.astype(F32)).astype(BF16)
    row = lax.broadcasted_iota(jnp.int32, (ts, ts), 0)
    col = lax.broadcasted_iota(jnp.int32, (ts, ts), 1)
    tri = jnp.where(col <= row, 1.0, 0.0).astype(BF16)
    cum = _dot(tri, p1) + _dot(tri, p2) + _dot(tri, p3) + carry_ref[...]
    o_ref[...] = cum
    carry_ref[...] = cum[ts - 1:ts, :]


def _forget_cumsum(rest, b_pad, batch, seq, col_block):
    T = rest.shape[0]
    ts = _pick(seq, (256, 128))
    ns = seq // ts
    return pl.pallas_call(
        _cum_kernel,
        grid=(batch, ns),
        in_specs=[
            pl.BlockSpec((ts, LANES), lambda b, s: (b * ns + s, col_block)),
            pl.BlockSpec((1, LANES), lambda b, s: (0, 0)),
        ],
        out_specs=pl.BlockSpec((ts, LANES), lambda b, s: (b * ns + s, 0)),
        out_shape=jax.ShapeDtypeStruct((T, LANES), F32),
        scratch_shapes=[pltpu.VMEM((1, LANES), F32)],
        compiler_params=_params(("arbitrary", "arbitrary")),
        name="forget_cumsum",
    )(rest, b_pad)


def _masked_heads(q_ref, pairs):
    lane = lax.broadcasted_iota(jnp.int32, (1, LANES), 1)
    heads = []
    for p in range(pairs):
        q2 = q_ref[:, p * LANES:(p + 1) * LANES] * jnp.asarray(HEAD_DIM ** -0.5, q_ref.dtype)
        zero = jnp.zeros_like(q2)
        heads += [jnp.where(lane < HEAD_DIM, q2, zero), jnp.where(lane >= HEAD_DIM, q2, zero)]
    return heads


def _pairs_per_step(cands, *counts):
    return next(c for c in cands if all(n % c == 0 for n in counts))


def _store_value_transposed(v_ref, vt_ref, first):
    pairs, nk, _, two_tk = vt_ref.shape
    tk = two_tk // 2
    for p in range(pairs):
        for j in range(nk):
            vt = v_ref[j * tk:(j + 1) * tk, p * LANES:(p + 1) * LANES].astype(F32).T
            both = jnp.concatenate([jnp.where(first, vt, 0.0), jnp.where(first, 0.0, vt)], axis=1)
            vt_ref[p, j] = both.astype(BF16)


def _sb_kernel(q_ref, k_ref, v_ref, o_ref, vt_ref, *, pairs):
    tq = q_ref.shape[0]
    i = pl.program_id(2)
    q_heads = _masked_heads(q_ref, pairs)
    first = lax.broadcasted_iota(jnp.int32, (LANES, 1), 0) < HEAD_DIM

    @pl.when(i == 0)
    def _():
        for p in range(pairs):
            for j in range(vt_ref.shape[1]):
                vt = v_ref[j * tq:(j + 1) * tq, p * LANES:(p + 1) * LANES].astype(F32).T
                vt_ref[p, j] = vt.astype(BF16)

    q_pair = [jnp.concatenate([q_heads[2 * p], q_heads[2 * p + 1]], axis=0) for p in range(pairs)]
    key = lax.broadcasted_iota(jnp.int32, (tq, tq), 0)
    qry = lax.broadcasted_iota(jnp.int32, (tq, tq), 1)
    later_keys = jnp.where(qry > key, 1.0, 0.0).astype(BF16)
    causal = jnp.concatenate([key < qry, key < qry], axis=1)

    def block(j, state, diagonal):
        start = pl.multiple_of(j * tq, tq)
        out = []
        scores = [_dot_nt(k_ref[pl.ds(start, tq), p * LANES:(p + 1) * LANES], q_pair[p])
                  for p in range(pairs)]
        log_sig, log_keep = [], []
        for z in scores:
            ls = jnp.minimum(z, 0.0) - jnp.log(1.0 + jnp.exp(-jnp.abs(z)))
            lk = ls - z
            if diagonal:
                lk = jnp.where(causal, lk, 0.0)
            log_sig.append(ls)
            log_keep.append(lk)
        later = []
        for lk in log_keep:
            lk_hi, lk_lo = _split2(lk)
            later.append(_dot(later_keys, lk_hi) + _dot(later_keys, lk_lo))
        weights = []
        for p in range(pairs):
            w = jnp.exp(log_sig[p] + later[p] + state[2 * p])
            if diagonal:
                w = jnp.where(causal, w, 0.0)
            weights.append(w.astype(BF16))
        for p in range(pairs):
            pv = _dot(vt_ref[p, j], weights[p])
            acc = state[2 * p + 1] + jnp.where(first, pv[:, :tq], pv[:, tq:])
            out += [state[2 * p] + jnp.sum(log_keep[p], axis=0, keepdims=True), acc]
        return tuple(out)

    def live(state):
        top = state[0]
        for p in range(1, pairs):
            top = jnp.maximum(top, state[2 * p])
        return (jnp.max(top) > SB_DEAD_LOG).astype(jnp.int32)

    init = (jnp.zeros((1, 2 * tq), F32), jnp.zeros((LANES, tq), F32)) * pairs
    state = block(i, init, True)

    def cond(carry):
        return (carry[0] >= 0) & (carry[1] > 0)

    def body(carry):
        state = block(carry[0], carry[2:], False)
        return (carry[0] - 1, live(state)) + state

    res = lax.while_loop(cond, body, (i - 1, live(state)) + state)
    slabs = [res[2 + 2 * p + 1].T for p in range(pairs)]
    o = slabs[0] if pairs == 1 else jnp.concatenate(slabs, axis=1)
    o_ref[...] = o.astype(o_ref.dtype)


def _sb_attention(qkv, batch, seq, n_pairs, q_col, k_col, v_col):
    T = qkv.shape[0]
    tq = _pick(seq, (256, 128))
    nq = seq // tq
    pairs = _pairs_per_step((4, 2, 1), n_pairs, q_col, k_col, v_col)
    W = pairs * LANES
    qc, kc, vc = q_col // pairs, k_col // pairs, v_col // pairs
    return pl.pallas_call(
        functools.partial(_sb_kernel, pairs=pairs),
        grid=(batch, n_pairs // pairs, nq),
        in_specs=[
            pl.BlockSpec((tq, W), lambda b, p, i: (b * nq + i, qc + p)),
            pl.BlockSpec((seq, W), lambda b, p, i: (b, kc + p)),
            pl.BlockSpec((seq, W), lambda b, p, i: (b, vc + p)),
        ],
        out_specs=pl.BlockSpec((tq, W), lambda b, p, i: (b * nq + i, p)),
        out_shape=jax.ShapeDtypeStruct((T, n_pairs * LANES), BF16),
        scratch_shapes=[pltpu.VMEM((pairs, nq, LANES, tq), BF16)],
        compiler_params=_params(("arbitrary", "arbitrary", "arbitrary")),
        name="sb_attention",
    )(qkv, qkv, qkv)


def _fox_kernel(q_ref, k_ref, v_ref, cc_ref, o_ref, vt_ref, *, pairs):
    tq = q_ref.shape[0]
    i = pl.program_id(2)
    q_heads = _masked_heads(q_ref, pairs)
    first = lax.broadcasted_iota(jnp.int32, (LANES, 1), 0) < HEAD_DIM

    @pl.when(i == 0)
    def _():
        _store_value_transposed(v_ref, vt_ref, first)

    key = lax.broadcasted_iota(jnp.int32, (tq, tq), 0)
    qry = lax.broadcasted_iota(jnp.int32, (tq, tq), 1)

    def block(j, carry, diagonal):
        start = pl.multiple_of(j * tq, tq)
        out = []
        scores = [_dot_nt(k_ref[pl.ds(start, tq), (h // 2) * LANES:(h // 2 + 1) * LANES], q_heads[h])
                  for h in range(2 * pairs)]
        shifted = []
        for h in range(2 * pairs):
            p, s = divmod(h, 2)
            logits = scores[h] - cc_ref[p, pl.ds(start, tq), s:s + 1]
            if diagonal:
                logits = jnp.where(key <= qry, logits, NEG_BIG)
            m_new = jnp.maximum(carry[5 * p + 2 * s], jnp.max(logits, axis=0, keepdims=True))
            shifted.append((logits, m_new))
        stats = []
        for h in range(2 * pairs):
            p, s = divmod(h, 2)
            logits, m_new = shifted[h]
            alpha = jnp.exp(carry[5 * p + 2 * s] - m_new)
            prob = jnp.exp(logits - m_new)
            l = carry[5 * p + 2 * s + 1] * alpha + jnp.sum(prob, axis=0, keepdims=True)
            stats.append((m_new, l, alpha, prob.astype(BF16)))
        for p in range(pairs):
            sa, sb = stats[2 * p], stats[2 * p + 1]
            pv = _dot(vt_ref[p, j], jnp.concatenate([sa[3], sb[3]], axis=0))
            acc = carry[5 * p + 4] * jnp.where(first, sa[2], sb[2]) + pv
            out += [sa[0], sa[1], sb[0], sb[1], acc]
        return tuple(out)

    row_stat = (jnp.full((1, tq), NEG_BIG, F32), jnp.zeros((1, tq), F32))
    init = (row_stat * 2 + (jnp.zeros((LANES, tq), F32),)) * pairs
    state = block(i, init, True)
    res = lax.fori_loop(0, i, lambda j, c: block(j, c, False), state)
    slabs = []
    for p in range(pairs):
        inv = jnp.where(first, 1.0 / res[5 * p + 1], 1.0 / res[5 * p + 3])
        slabs.append((res[5 * p + 4] * inv).T)
    o = slabs[0] if pairs == 1 else jnp.concatenate(slabs, axis=1)
    o_ref[...] = o.astype(o_ref.dtype)


def _fox_attention(qkv, cum_cols, batch, seq, tq, n_pairs, q_col, k_col, v_col):
    T = qkv.shape[0]
    nq = seq // tq
    pairs = _pairs_per_step((FOX_PAIRS, 1), n_pairs, q_col, k_col, v_col)
    W = pairs * LANES
    qc, kc, vc = q_col // pairs, k_col // pairs, v_col // pairs
    return pl.pallas_call(
        functools.partial(_fox_kernel, pairs=pairs),
        grid=(batch, n_pairs // pairs, nq),
        in_specs=[
            pl.BlockSpec((tq, W), lambda b, p, i: (b * nq + i, qc + p)),
            pl.BlockSpec((seq, W), lambda b, p, i: (b, kc + p)),
            pl.BlockSpec((seq, W), lambda b, p, i: (b, vc + p)),
            pl.BlockSpec((pairs, seq, 2), lambda b, p, i: (p, b, 0)),
        ],
        out_specs=pl.BlockSpec((tq, W), lambda b, p, i: (b * nq + i, p)),
        out_shape=jax.ShapeDtypeStruct((T, n_pairs * LANES), BF16),
        scratch_shapes=[pltpu.VMEM((pairs, nq, LANES, 2 * tq), BF16)],
        compiler_params=_params(("arbitrary", "arbitrary", "arbitrary")),
        name="fox_attention",
    )(qkv, qkv, qkv, cum_cols)


def _conv_kernel(a_ref, g_ref, w_ref, cb_ref, lg_ref, lb_ref, o_ref, u_ref, s_ref, *, width, halo):
    ts = a_ref.shape[0]

    @pl.when(pl.program_id(1) == 0)
    def _():
        u_ref[0:halo, :] = jnp.zeros((halo, u_ref.shape[1]), F32)

    u_ref[halo:halo + ts, :] = a_ref[...] * _sigmoid(g_ref[...])
    span = ts + halo - SUBLANES
    for r in range(1, SUBLANES):
        s_ref[r - 1, 0:span, :] = u_ref[r:r + span, :]
    acc = jnp.zeros(a_ref.shape, F32) + cb_ref[...]
    for k in range(width):
        base, r = divmod(halo - (width - 1) + k, SUBLANES)
        lo = base * SUBLANES
        tap = u_ref[lo:lo + ts, :] if r == 0 else s_ref[r - 1, lo:lo + ts, :]
        acc = acc + w_ref[k:k + 1, :] * tap
    y = _ln_rows(acc) * lg_ref[...] + lb_ref[...]
    o_ref[...] = (y * _sigmoid(y)).astype(o_ref.dtype)
    u_ref[0:halo, :] = u_ref[ts:ts + halo, :]


def _conv_module(rest, conv_w, conv_b, ln_g, ln_b, batch, seq):
    T = rest.shape[0]
    width, C = conv_w.shape
    halo = -(-(width - 1) // SUBLANES) * SUBLANES
    ts = _pick(seq, (128,))
    ns = seq // ts
    kern = functools.partial(_conv_kernel, width=width, halo=halo)
    vec = lambda: pl.BlockSpec((1, C), lambda b, s: (0, 0))
    return pl.pallas_call(
        kern,
        grid=(batch, ns),
        in_specs=[
            pl.BlockSpec((ts, C), lambda b, s: (b * ns + s, 0)),
            pl.BlockSpec((ts, C), lambda b, s: (b * ns + s, 1)),
            pl.BlockSpec((width, C), lambda b, s: (0, 0)),
            vec(), vec(), vec(),
        ],
        out_specs=pl.BlockSpec((ts, C), lambda b, s: (b * ns + s, 0)),
        out_shape=jax.ShapeDtypeStruct((T, C), BF16),
        scratch_shapes=[pltpu.VMEM((ts + halo, C), F32),
                        pltpu.VMEM((SUBLANES - 1, ts + halo, C), F32)],
        compiler_params=_params(("arbitrary", "arbitrary")),
        name="conformer_conv",
    )(rest, rest, conv_w, conv_b.reshape(1, C), ln_g.reshape(1, C), ln_b.reshape(1, C))


def _outproj_kernel(osb_ref, ofx_ref, ocv_ref, w_ref, x_ref, gt_ref, g_ref, b_ref,
                    sc_ref, sh_ref, rwh_ref, rwl_ref, rb_ref,
                    x1_ref, h2_ref, lg_ref, *, alpha):
    d_sb = osb_ref.shape[1]
    d_fx = ofx_ref.shape[1]
    y = _dot(osb_ref[...], w_ref[0:d_sb, :])
    y = y + _dot(ofx_ref[...], w_ref[d_sb:d_sb + d_fx, :])
    y = y + _dot(ocv_ref[...], w_ref[d_sb + d_fx:, :])
    x1 = _ln_rows(alpha * x_ref[...] + gt_ref[0] * y) * g_ref[...] + b_ref[...]
    x1_ref[...] = x1
    h2 = _ln_rows(x1) * (1.0 + sc_ref[0]) + sh_ref[0]
    h2_ref[...] = h2
    h_hi, h_lo = _split2(h2)
    lg = _dot(h_hi, rwh_ref[...]) + _dot(h_lo, rwh_ref[...]) + _dot(h_hi, rwl_ref[...])
    lg_ref[...] = lg + rb_ref[...]


def _outproj(o_sb, o_fx, o_cv, w_out, x2d, gt, g, b, sc2, sh2, rw_hi, rw_lo, rb, seq, alpha):
    T, D = x2d.shape
    tm = _pick(seq, (256, 128))
    per_b = seq // tm
    rowblk = lambda n: pl.BlockSpec((tm, n), lambda i: (i, 0))
    full = lambda r, c: pl.BlockSpec((r, c), lambda i: (0, 0))
    perb = lambda: pl.BlockSpec((1, 1, D), lambda i: (i // per_b, 0, 0))
    return pl.pallas_call(
        functools.partial(_outproj_kernel, alpha=alpha),
        grid=(T // tm,),
        in_specs=[
            rowblk(o_sb.shape[1]), rowblk(o_fx.shape[1]), rowblk(o_cv.shape[1]),
            full(D, D), rowblk(D), perb(), full(1, D), full(1, D), perb(), perb(),
            full(D, LANES), full(D, LANES), full(1, LANES),
        ],
        out_specs=[rowblk(D), rowblk(D), rowblk(LANES)],
        out_shape=[jax.ShapeDtypeStruct((T, D), F32), jax.ShapeDtypeStruct((T, D), F32),
                   jax.ShapeDtypeStruct((T, LANES), F32)],
        compiler_params=_params(("arbitrary",)),
        name="out_proj_norm_router",
    )(o_sb, o_fx, o_cv, w_out, x2d, gt, g, b, sc2, sh2, rw_hi, rw_lo, rb)


def _route_kernel(lg_ref, e_ref, w_ref, *, n_groups, epg):
    L = lg_ref[...]
    lane_i = lax.broadcasted_iota(jnp.int32, L.shape, 1)
    lane = lane_i.astype(F32)
    far = 1e6
    is_grp = lane < n_groups
    l1 = jnp.where(is_grp, L, NEG_BIG)
    m1 = jnp.max(l1, axis=1, keepdims=True)
    grp = jnp.min(jnp.where(is_grp & (l1 == m1), lane, far), axis=1, keepdims=True)
    s1 = jnp.sum(jnp.where(is_grp, jnp.exp(l1 - m1), 0.0), axis=1, keepdims=True)
    p_grp = 1.0 / s1
    lo = n_groups + grp * epg
    in_grp = (lane >= lo) & (lane < lo + epg)
    l2 = jnp.where(in_grp, L, NEG_BIG)
    m2 = jnp.max(l2, axis=1, keepdims=True)
    e2 = jnp.where(in_grp, jnp.exp(l2 - m2), -1.0)
    v1 = jnp.max(e2, axis=1, keepdims=True)
    i1 = jnp.min(jnp.where(e2 == v1, lane, far), axis=1, keepdims=True)
    e2b = jnp.where(lane == i1, -1.0, e2)
    v2 = jnp.max(e2b, axis=1, keepdims=True)
    i2 = jnp.min(jnp.where(e2b == v2, lane, far), axis=1, keepdims=True)
    den = v1 + v2
    w0 = p_grp * v1 / den
    w1 = p_grp * v2 / den
    ids = jnp.where(lane_i == 0, i1 - n_groups, jnp.where(lane_i == 1, i2 - n_groups, 0.0))
    e_ref[...] = ids.astype(jnp.int32)
    w_ref[...] = jnp.where(lane_i == 0, w0, jnp.where(lane_i == 1, w1, 0.0))


def _route(logits, n_groups, epg):
    T = logits.shape[0]
    tm = _pick(T, (512, 256, 128))
    blk = lambda: pl.BlockSpec((tm, LANES), lambda i: (i, 0))
    return pl.pallas_call(
        functools.partial(_route_kernel, n_groups=n_groups, epg=epg),
        grid=(T // tm,),
        in_specs=[blk()],
        out_specs=[blk(), blk()],
        out_shape=[jax.ShapeDtypeStruct((T, LANES), jnp.int32),
                   jax.ShapeDtypeStruct((T, LANES), F32)],
        compiler_params=_params(("arbitrary",)),
        name="route_topk",
    )(logits)


def _plan_kernel(e_ref, dest_ref, be_ref, cnt_ref, base_ref, *, n_experts):
    p = pl.program_id(0)
    i = pl.program_id(1)
    tm = e_ref.shape[0]
    lane = lax.broadcasted_iota(jnp.int32, (tm, LANES), 1)
    e = e_ref[...]
    hot0 = lane == e[:, 0:1]
    hot1 = lane == e[:, 1:2]
    both = jnp.where(hot0 | hot1, 1.0, 0.0)

    @pl.when((p == 0) & (i == 0))
    def _():
        cnt_ref[...] = jnp.zeros_like(cnt_ref)

    @pl.when(p == 0)
    def _():
        cnt_ref[...] += jnp.sum(both, axis=0, keepdims=True)

    @pl.when((p == 1) & (i == 0))
    def _():
        nblk = jnp.floor((cnt_ref[...] + (MOE_BLOCK - 1)) * (1.0 / MOE_BLOCK))
        r = lax.broadcasted_iota(jnp.int32, (LANES, LANES), 0)
        c = lax.broadcasted_iota(jnp.int32, (LANES, LANES), 1)
        before = jnp.where(r < c, 1.0, 0.0).astype(BF16)
        nb8 = jnp.broadcast_to(nblk, (SUBLANES, LANES)).astype(BF16)
        excl = _dot(nb8, before)[0:1, :]
        base_ref[...] = excl * MOE_BLOCK
        cnt_ref[...] = jnp.zeros_like(cnt_ref)
        incl = excl + nblk
        nb_rows = be_ref.shape[0]
        bidx = lax.broadcasted_iota(jnp.int32, (nb_rows, LANES), 0).astype(F32)
        lane_b = lax.broadcasted_iota(jnp.int32, (nb_rows, LANES), 1)
        done = jnp.where((incl <= bidx) & (lane_b < n_experts), 1.0, 0.0)
        be = jnp.sum(done, axis=1, keepdims=True)
        be_ref[...] = jnp.broadcast_to(be, (nb_rows, LANES)).astype(jnp.int32)

    @pl.when(p == 1)
    def _():
        r = lax.broadcasted_iota(jnp.int32, (tm, tm), 0)
        c = lax.broadcasted_iota(jnp.int32, (tm, tm), 1)
        earlier = jnp.where(c < r, 1.0, 0.0).astype(BF16)
        tot = _dot(earlier, both.astype(BF16)) + cnt_ref[...] + base_ref[...]
        d0 = jnp.sum(jnp.where(hot0, tot, 0.0), axis=1, keepdims=True)
        d1 = jnp.sum(jnp.where(hot1, tot, 0.0), axis=1, keepdims=True)
        dest = jnp.where(lane == 0, d0, jnp.where(lane == 1, d1, 0.0))
        dest_ref[...] = dest.astype(jnp.int32)
        cnt_ref[...] += jnp.sum(both, axis=0, keepdims=True)


def _plan(e_lanes, n_experts, n_blocks):
    T = e_lanes.shape[0]
    tm = _pick(T, (256, 128))
    nb_rows = -(-n_blocks // SUBLANES) * SUBLANES
    return pl.pallas_call(
        functools.partial(_plan_kernel, n_experts=n_experts),
        grid=(2, T // tm),
        in_specs=[pl.BlockSpec((tm, LANES), lambda p, i: (i, 0))],
        out_specs=[pl.BlockSpec((tm, LANES), lambda p, i: (i * p, 0)),
                   pl.BlockSpec((nb_rows, LANES), lambda p, i: (0, 0))],
        out_shape=[jax.ShapeDtypeStruct((T, LANES), jnp.int32),
                   jax.ShapeDtypeStruct((nb_rows, LANES), jnp.int32)],
        scratch_shapes=[pltpu.VMEM((1, LANES), F32), pltpu.VMEM((1, LANES), F32)],
        compiler_params=_params(("arbitrary", "arbitrary")),
        name="dispatch_plan",
    )(e_lanes)


def _row_copy(src, src_row, dst, dst_row, sem):
    return pltpu.make_async_copy(src.at[pl.ds(src_row, 1)], dst.at[pl.ds(dst_row, 1)], sem)


def _slotmap_kernel(dest_ref, tok_ref):
    def clear(s, carry):
        tok_ref[s] = 0
        return carry

    lax.fori_loop(0, tok_ref.shape[0], clear, 0, unroll=8)

    def put(t, carry):
        tok_ref[dest_ref[2 * t]] = t
        tok_ref[dest_ref[2 * t + 1]] = t
        return carry

    lax.fori_loop(0, dest_ref.shape[0] // 2, put, 0, unroll=8)


def _slot_map(dest_flat, n_slots):
    return pl.pallas_call(
        _slotmap_kernel,
        in_specs=[pl.BlockSpec(memory_space=pltpu.SMEM)],
        out_specs=pl.BlockSpec(memory_space=pltpu.SMEM),
        out_shape=jax.ShapeDtypeStruct((n_slots,), jnp.int32),
        name="moe_slot_map",
    )(dest_flat)


def _expert_kernel(be_ref, nu_ref, tok_ref, h_ref, wg_ref, wu_ref, wd_ref, o_ref, xbuf, sems):
    del be_ref
    b = pl.program_id(0)
    n_used = nu_ref[0]
    slot = b % 2

    def issue(blk, s, r):
        _row_copy(h_ref, tok_ref[blk * MOE_BLOCK + r], xbuf.at[s], r, sems.at[s]).start()

    @pl.when(b == 0)
    def _():
        def step(r, carry):
            issue(0, 0, r)
            return carry

        lax.fori_loop(0, MOE_BLOCK, step, 0, unroll=8)

    @pl.when(b + 1 < n_used)
    def _():
        for r in range(MOE_BLOCK):
            issue(b + 1, 1 - slot, r)

    @pl.when(b < n_used)
    def _():
        def drain(r, carry):
            _row_copy(h_ref, 0, xbuf.at[slot], 0, sems.at[slot]).wait()
            return carry

        lax.fori_loop(0, MOE_BLOCK, drain, 0, unroll=8)
        x = xbuf[slot].astype(BF16)
        g = _dot(x, wg_ref[0, 0])
        u = _dot(x, wu_ref[0, 0])
        hid = (g * _sigmoid(g)) * u
        o_ref[...] = _dot(hid.astype(BF16), wd_ref[0, 0])

    @pl.when(b >= n_used)
    def _():
        o_ref[...] = jnp.zeros_like(o_ref)


def _experts(block_expert, n_used, slot_tok, h2, w_gate, w_up, w_down, layer, n_blocks):
    D = h2.shape[1]
    DE = w_gate.shape[3]
    wmap = lambda b, be, nu, tok: (layer, be[jnp.minimum(b, nu[0] - 1)], 0, 0)
    return pl.pallas_call(
        _expert_kernel,
        grid_spec=pltpu.PrefetchScalarGridSpec(
            num_scalar_prefetch=3,
            grid=(n_blocks,),
            in_specs=[
                pl.BlockSpec(memory_space=pl.ANY),
                pl.BlockSpec((1, 1, D, DE), wmap),
                pl.BlockSpec((1, 1, D, DE), wmap),
                pl.BlockSpec((1, 1, DE, D), wmap),
            ],
            out_specs=pl.BlockSpec((MOE_BLOCK, D), lambda b, be, nu, tok: (b, 0)),
            scratch_shapes=[pltpu.VMEM((2, MOE_BLOCK, D), F32), pltpu.SemaphoreType.DMA((2,))],
        ),
        out_shape=jax.ShapeDtypeStruct((n_blocks * MOE_BLOCK, D), F32),
        compiler_params=_params(("arbitrary",)),
        name="moe_experts",
    )(block_expert, n_used, slot_tok, h2, w_gate, w_up, w_down)


def _combine_kernel(dest_ref, yb_ref, w_ref, x_ref, gt_ref, g_ref, b_ref, o_ref,
                    buf_ref, sems, *, alpha):
    tm = x_ref.shape[0]
    i = pl.program_id(0)
    slot = i % 2

    def issue(tile, s, r):
        t = tile * tm + r
        _row_copy(yb_ref, dest_ref[2 * t], buf_ref.at[s, 0], r, sems.at[s]).start()
        _row_copy(yb_ref, dest_ref[2 * t + 1], buf_ref.at[s, 1], r, sems.at[s]).start()

    @pl.when(i == 0)
    def _():
        def step(r, carry):
            issue(0, 0, r)
            return carry

        lax.fori_loop(0, tm, step, 0, unroll=8)

    @pl.when(i + 1 < pl.num_programs(0))
    def _():
        for r in range(tm):
            issue(i + 1, 1 - slot, r)

    def drain(r, carry):
        _row_copy(yb_ref, 0, buf_ref.at[slot, 0], 0, sems.at[slot]).wait()
        return carry

    lax.fori_loop(0, 2 * tm, drain, 0, unroll=8)
    w = w_ref[...]
    y = w[:, 0:1] * buf_ref[slot, 0] + w[:, 1:2] * buf_ref[slot, 1]
    v = alpha * x_ref[...] + gt_ref[0] * y
    o_ref[...] = _ln_rows(v) * g_ref[...] + b_ref[...]


def _combine(dest_flat, yb, w_lanes, x1, gt, g, b, seq, alpha):
    T, D = x1.shape
    tm = _pick(seq, (256, 128))
    per_b = seq // tm
    return pl.pallas_call(
        functools.partial(_combine_kernel, alpha=alpha),
        grid_spec=pltpu.PrefetchScalarGridSpec(
            num_scalar_prefetch=1,
            grid=(T // tm,),
            in_specs=[
                pl.BlockSpec(memory_space=pl.ANY),
                pl.BlockSpec((tm, LANES), lambda i, d: (i, 0)),
                pl.BlockSpec((tm, D), lambda i, d: (i, 0)),
                pl.BlockSpec((1, 1, D), lambda i, d: (i // per_b, 0, 0)),
                pl.BlockSpec((1, D), lambda i, d: (0, 0)),
                pl.BlockSpec((1, D), lambda i, d: (0, 0)),
            ],
            out_specs=pl.BlockSpec((tm, D), lambda i, d: (i, 0)),
            scratch_shapes=[pltpu.VMEM((2, 2, tm, D), F32), pltpu.SemaphoreType.DMA((2,))],
        ),
        out_shape=jax.ShapeDtypeStruct((T, D), F32),
        compiler_params=_params(("arbitrary",)),
        name="moe_combine_norm",
    )(dest_flat, yb, w_lanes, x1, gt, g, b)


def kernel(x, c, ada_w, ada_b, w_in, b_forget, conv_w, conv_b, conv_ln_g, conv_ln_b, w_out,
           ln1_g, ln1_b, r1_w, r1_b, r2_w, r2_b, w_gate, w_up, w_down, ln2_g, ln2_b):
    B, S, D = x.shape
    L = ada_w.shape[0]
    T = B * S
    alpha = float((2 * L) ** 0.25)
    d_sb, d_fx, c_cv = D // 4, D // 2, D // 4
    n_fx = d_fx // HEAD_DIM
    n_groups = r1_w.shape[-1]
    epg = r2_w.shape[-1]
    n_experts = n_groups * epg
    n_blocks = (2 * T) // MOE_BLOCK + n_experts
    n_slots = n_blocks * MOE_BLOCK
    qkv_cols = 3 * d_sb + 3 * d_fx
    tq = _pick(S, (512, 256, 128))

    rows = -(-B // SUBLANES) * SUBLANES
    c_pad = jnp.zeros((rows, D), F32).at[:B].set(c)
    mod_all = _ada(c_pad, ada_w, ada_b)

    wg16, wu16, wd16 = w_gate.astype(BF16), w_up.astype(BF16), w_down.astype(BF16)
    x2d = x.reshape(T, D)
    for l in range(L):
        mod = mod_all[l, :B]
        sh1, sc1, gt1, sh2, sc2, gt2 = [m.reshape(B, 1, D) for m in jnp.split(mod, 6, axis=-1)]

        h1 = _lnmod(x2d, sc1, sh1, S)
        w_qkv = w_in[l, :, :qkv_cols].astype(BF16)
        f_lo = qkv_cols
        g_lo = qkv_cols + n_fx
        w_rest = jnp.concatenate(
            [w_in[l, :, g_lo:g_lo + 2 * c_cv], w_in[l, :, f_lo:f_lo + n_fx],
             jnp.zeros((D, LANES - n_fx), F32)], axis=1).astype(BF16)
        qkv = _matmul(h1, w_qkv, BF16, "proj_qkv")
        rest = _matmul(h1, w_rest, F32, "proj_glu_forget")

        b_pad = jnp.zeros((1, LANES), F32).at[0, :n_fx].set(b_forget[l])
        cum = _forget_cumsum(rest, b_pad, B, S, (2 * c_cv) // LANES)[:, :n_fx]
        cum_cols = cum.reshape(T, n_fx // 2, 2).transpose(1, 0, 2)

        nb = LANES
        o_sb = _sb_attention(qkv, B, S, d_sb // nb, 0, d_sb // nb, 2 * d_sb // nb)
        fx0 = 3 * d_sb // nb
        o_fx = _fox_attention(qkv, cum_cols, B, S, tq, d_fx // nb, fx0, fx0 + d_fx // nb,
                              fx0 + 2 * d_fx // nb)
        o_cv = _conv_module(rest, conv_w[l], conv_b[l], conv_ln_g[l], conv_ln_b[l], B, S)

        rw = jnp.concatenate(
            [r1_w[l], r2_w[l].transpose(1, 0, 2).reshape(D, n_experts),
             jnp.zeros((D, LANES - n_groups - n_experts), F32)], axis=1)
        rw_hi = rw.astype(BF16)
        rw_lo = (rw - rw_hi.astype(F32)).astype(BF16)
        rb = jnp.zeros((1, LANES), F32).at[0, :n_groups].set(r1_b[l])
        rb = rb.at[0, n_groups:n_groups + n_experts].set(r2_b[l].reshape(-1))
        x1, h2, logits = _outproj(
            o_sb, o_fx, o_cv, w_out[l].astype(BF16), x2d, gt1,
            ln1_g[l].reshape(1, D), ln1_b[l].reshape(1, D), sc2, sh2, rw_hi, rw_lo, rb, S, alpha)

        e_lanes, w_lanes = _route(logits, n_groups, epg)
        dest_lanes, be_lanes = _plan(e_lanes, n_experts, n_blocks)
        dest_flat = dest_lanes[:, :2].reshape(-1)
        be = be_lanes[:n_blocks, 0]
        n_used = jnp.sum((be < n_experts).astype(jnp.int32)).reshape(1)
        block_expert = jnp.minimum(be, n_experts - 1)
        slot_tok = _slot_map(dest_flat, n_slots)
        yb = _experts(block_expert, n_used, slot_tok, h2, wg16, wu16, wd16, l, n_blocks)
        x2d = _combine(dest_flat, yb, w_lanes, x1, gt2,
                       ln2_g[l].reshape(1, D), ln2_b[l].reshape(1, D), S, alpha)
    return x2d.reshape(B, S, D)
```

```python
import functools

import jax
import jax.numpy as jnp
from jax import lax
from jax.experimental import pallas as pl
from jax.experimental.pallas import tpu as pltpu

LN_EPS = 1e-5
HEAD_DIM = 64
LANES = 128
SUBLANES = 8
MOE_BLOCK = 256
NEG_BIG = -1e30
SB_DEAD_LOG = -104.0
FOX_PAIRS = 2
VMEM_LIMIT = 56 * 1024 * 1024

F32 = jnp.float32
BF16 = jnp.bfloat16


def _pick(n, cands):
    for c in cands:
        if n % c == 0:
            return c
    return n


def _params(sem):
    return pltpu.CompilerParams(dimension_semantics=sem, vmem_limit_bytes=VMEM_LIMIT)


def _ln_rows(v):
    mu = jnp.mean(v, axis=-1, keepdims=True)
    d = v - mu
    var = jnp.mean(d * d, axis=-1, keepdims=True)
    return d * lax.rsqrt(var + LN_EPS)


def _log_sigmoid(z):
    return jnp.minimum(z, 0.0) - jnp.log1p(jnp.exp(-jnp.abs(z)))


def _sigmoid(z):
    return 1.0 / (1.0 + jnp.exp(-z))


def _split2(v):
    hi = v.astype(BF16)
    lo = (v - hi.astype(F32)).astype(BF16)
    return hi, lo


def _dot(a, b):
    return jnp.dot(a, b, preferred_element_type=F32)


def _dot_nt(a, b):
    return lax.dot_general(a, b, (((1,), (1,)), ((), ())), preferred_element_type=F32)


def _ada_kernel(c_ref, w_ref, b_ref, o_ref):
    c = c_ref[...]
    s = c * _sigmoid(c)
    s_hi, s_lo = _split2(s)
    w_hi, w_lo = _split2(w_ref[0])
    acc = _dot(s_hi, w_hi) + _dot(s_lo, w_hi) + _dot(s_hi, w_lo)
    o_ref[0] = acc + b_ref[0]


def _ada(c_pad, ada_w, ada_b):
    L, D, N = ada_w.shape
    rows = c_pad.shape[0]
    tn = _pick(N, (512, 256, 128))
    return pl.pallas_call(
        _ada_kernel,
        grid=(L, N // tn),
        in_specs=[
            pl.BlockSpec((rows, D), lambda l, n: (0, 0)),
            pl.BlockSpec((1, D, tn), lambda l, n: (l, 0, n)),
            pl.BlockSpec((1, 1, tn), lambda l, n: (l, 0, n)),
        ],
        out_specs=pl.BlockSpec((1, rows, tn), lambda l, n: (l, 0, n)),
        out_shape=jax.ShapeDtypeStruct((L, rows, N), F32),
        compiler_params=_params(("arbitrary", "arbitrary")),
        name="ada_mod",
    )(c_pad, ada_w, ada_b.reshape(L, 1, N))


def _lnmod_kernel(x_ref, sc_ref, sh_ref, o_ref):
    h = _ln_rows(x_ref[...]) * (1.0 + sc_ref[0]) + sh_ref[0]
    o_ref[...] = h.astype(o_ref.dtype)


def _lnmod(x2d, sc, sh, seq):
    T, D = x2d.shape
    tm = _pick(seq, (512, 256, 128))
    per_b = seq // tm
    return pl.pallas_call(
        _lnmod_kernel,
        grid=(T // tm,),
        in_specs=[
            pl.BlockSpec((tm, D), lambda i: (i, 0)),
            pl.BlockSpec((1, 1, D), lambda i: (i // per_b, 0, 0)),
            pl.BlockSpec((1, 1, D), lambda i: (i // per_b, 0, 0)),
        ],
        out_specs=pl.BlockSpec((tm, D), lambda i: (i, 0)),
        out_shape=jax.ShapeDtypeStruct((T, D), BF16),
        compiler_params=_params(("arbitrary",)),
        name="ln_mod",
    )(x2d, sc, sh)


def _mm_kernel(a_ref, b_ref, o_ref):
    o_ref[...] = _dot(a_ref[...], b_ref[...]).astype(o_ref.dtype)


def _matmul(a, w, out_dtype, name):
    M, K = a.shape
    N = w.shape[1]
    tm = _pick(M, (1024, 512, 256, 128))
    tn = _pick(N, (512, 384, 256, 128))
    return pl.pallas_call(
        _mm_kernel,
        grid=(M // tm, N // tn),
        in_specs=[
            pl.BlockSpec((tm, K), lambda i, j: (i, 0)),
            pl.BlockSpec((K, tn), lambda i, j: (0, j)),
        ],
        out_specs=pl.BlockSpec((tm, tn), lambda i, j: (i, j)),
        out_shape=jax.ShapeDtypeStruct((M, N), out_dtype),
        compiler_params=_params(("arbitrary", "arbitrary")),
        name=name,
    )(a, w)


def _cum_kernel(f_ref, b_ref, o_ref, carry_ref):
    @pl.when(pl.program_id(1) == 0)
    def _():
        carry_ref[...] = jnp.zeros_like(carry_ref)

    ts = f_ref.shape[0]
    lf = _log_sigmoid(f_ref[...] + b_ref[...])
    p1 = lf.astype(BF16)
    r1 = lf - p1.astype(F32)
    p2 = r1.astype(BF16)
    p3 = (r1 - p2.astype(F32)).astype(BF16)
    row = lax.broadcasted_iota(jnp.int32, (ts, ts), 0)
    col = lax.broadcasted_iota(jnp.int32, (ts, ts), 1)
    tri = jnp.where(col <= row, 1.0, 0.0).astype(BF16)
    cum = _dot(tri, p1) + _dot(tri, p2) + _dot(tri, p3) + carry_ref[...]
    o_ref[...] = cum
    carry_ref[...] = cum[ts - 1:ts, :]


def _forget_cumsum(rest, b_pad, batch, seq, col_block):
    T = rest.shape[0]
    ts = _pick(seq, (256, 128))
    ns = seq // ts
    return pl.pallas_call(
        _cum_kernel,
        grid=(batch, ns),
        in_specs=[
            pl.BlockSpec((ts, LANES), lambda b, s: (b * ns + s, col_block)),
            pl.BlockSpec((1, LANES), lambda b, s: (0, 0)),
        ],
        out_specs=pl.BlockSpec((ts, LANES), lambda b, s: (b * ns + s, 0)),
        out_shape=jax.ShapeDtypeStruct((T, LANES), F32),
        scratch_shapes=[pltpu.VMEM((1, LANES), F32)],
        compiler_params=_params(("arbitrary", "arbitrary")),
        name="forget_cumsum",
    )(rest, b_pad)


def _masked_heads(q_ref, pairs):
    lane = lax.broadcasted_iota(jnp.int32, (1, LANES), 1)
    heads = []
    for p in range(pairs):
        q2 = q_ref[:, p * LANES:(p + 1) * LANES] * jnp.asarray(HEAD_DIM ** -0.5, q_ref.dtype)
        zero = jnp.zeros_like(q2)
        heads += [jnp.where(lane < HEAD_DIM, q2, zero), jnp.where(lane >= HEAD_DIM, q2, zero)]
    return heads


def _pairs_per_step(cands, *counts):
    return next(c for c in cands if all(n % c == 0 for n in counts))


def _store_value_transposed(v_ref, vt_ref, first):
    pairs, nk, _, two_tk = vt_ref.shape
    tk = two_tk // 2
    for p in range(pairs):
        for j in range(nk):
            vt = v_ref[j * tk:(j + 1) * tk, p * LANES:(p + 1) * LANES].astype(F32).T
            both = jnp.concatenate([jnp.where(first, vt, 0.0), jnp.where(first, 0.0, vt)], axis=1)
            vt_ref[p, j] = both.astype(BF16)


def _sb_kernel(q_ref, k_ref, v_ref, o_ref, vt_ref, *, pairs):
    tq = q_ref.shape[0]
    i = pl.program_id(2)
    q_heads = _masked_heads(q_ref, pairs)
    first = lax.broadcasted_iota(jnp.int32, (LANES, 1), 0) < HEAD_DIM

    @pl.when(i == 0)
    def _():
        for p in range(pairs):
            for j in range(vt_ref.shape[1]):
                vt = v_ref[j * tq:(j + 1) * tq, p * LANES:(p + 1) * LANES].astype(F32).T
                vt_ref[p, j] = vt.astype(BF16)

    q_pair = [jnp.concatenate([q_heads[2 * p], q_heads[2 * p + 1]], axis=0) for p in range(pairs)]
    key = lax.broadcasted_iota(jnp.int32, (tq, tq), 0)
    qry = lax.broadcasted_iota(jnp.int32, (tq, tq), 1)
    later_keys = jnp.where(qry > key, 1.0, 0.0).astype(BF16)
    causal = jnp.concatenate([key < qry, key < qry], axis=1)

    def block(j, state, diagonal):
        start = pl.multiple_of(j * tq, tq)
        out = []
        scores = [_dot_nt(k_ref[pl.ds(start, tq), p * LANES:(p + 1) * LANES], q_pair[p])
                  for p in range(pairs)]
        log_sig, log_keep = [], []
        for z in scores:
            ls = jnp.minimum(z, 0.0) - jnp.log(1.0 + jnp.exp(-jnp.abs(z)))
            lk = ls - z
            if diagonal:
                lk = jnp.where(causal, lk, 0.0)
            log_sig.append(ls)
            log_keep.append(lk)
        later = []
        for lk in log_keep:
            lk_hi, lk_lo = _split2(lk)
            later.append(_dot(later_keys, lk_hi) + _dot(later_keys, lk_lo))
        weights = []
        for p in range(pairs):
            w = jnp.exp(log_sig[p] + later[p] + state[2 * p])
            if diagonal:
                w = jnp.where(causal, w, 0.0)
            weights.append(w.astype(BF16))
        for p in range(pairs):
            pv = _dot(vt_ref[p, j], weights[p])
            acc = state[2 * p + 1] + jnp.where(first, pv[:, :tq], pv[:, tq:])
            out += [state[2 * p] + jnp.sum(log_keep[p], axis=0, keepdims=True), acc]
        return tuple(out)

    def live(state):
        top = state[0]
        for p in range(1, pairs):
            top = jnp.maximum(top, state[2 * p])
        return (jnp.max(top) > SB_DEAD_LOG).astype(jnp.int32)

    init = (jnp.zeros((1, 2 * tq), F32), jnp.zeros((LANES, tq), F32)) * pairs
    state = block(i, init, True)

    def cond(carry):
        return (carry[0] >= 0) & (carry[1] > 0)

    def body(carry):
        state = block(carry[0], carry[2:], False)
        return (carry[0] - 1, live(state)) + state

    res = lax.while_loop(cond, body, (i - 1, live(state)) + state)
    slabs = [res[2 + 2 * p + 1].T for p in range(pairs)]
    o = slabs[0] if pairs == 1 else jnp.concatenate(slabs, axis=1)
    o_ref[...] = o.astype(o_ref.dtype)


def _sb_attention(qkv, batch, seq, n_pairs, q_col, k_col, v_col):
    T = qkv.shape[0]
    tq = _pick(seq, (256, 128))
    nq = seq // tq
    pairs = _pairs_per_step((4, 2, 1), n_pairs, q_col, k_col, v_col)
    W = pairs * LANES
    qc, kc, vc = q_col // pairs, k_col // pairs, v_col // pairs
    return pl.pallas_call(
        functools.partial(_sb_kernel, pairs=pairs),
        grid=(batch, n_pairs // pairs, nq),
        in_specs=[
            pl.BlockSpec((tq, W), lambda b, p, i: (b * nq + i, qc + p)),
            pl.BlockSpec((seq, W), lambda b, p, i: (b, kc + p)),
            pl.BlockSpec((seq, W), lambda b, p, i: (b, vc + p)),
        ],
        out_specs=pl.BlockSpec((tq, W), lambda b, p, i: (b * nq + i, p)),
        out_shape=jax.ShapeDtypeStruct((T, n_pairs * LANES), BF16),
        scratch_shapes=[pltpu.VMEM((pairs, nq, LANES, tq), BF16)],
        compiler_params=_params(("arbitrary", "arbitrary", "arbitrary")),
        name="sb_attention",
    )(qkv, qkv, qkv)


def _fox_kernel(q_ref, k_ref, v_ref, cc_ref, o_ref, vt_ref, *, pairs):
    tq = q_ref.shape[0]
    i = pl.program_id(2)
    q_heads = _masked_heads(q_ref, pairs)
    first = lax.broadcasted_iota(jnp.int32, (LANES, 1), 0) < HEAD_DIM

    @pl.when(i == 0)
    def _():
        _store_value_transposed(v_ref, vt_ref, first)

    key = lax.broadcasted_iota(jnp.int32, (tq, tq), 0)
    qry = lax.broadcasted_iota(jnp.int32, (tq, tq), 1)

    def block(j, carry, diagonal):
        start = pl.multiple_of(j * tq, tq)
        out = []
        scores = [_dot_nt(k_ref[pl.ds(start, tq), (h // 2) * LANES:(h // 2 + 1) * LANES], q_heads[h])
                  for h in range(2 * pairs)]
        shifted = []
        for h in range(2 * pairs):
            p, s = divmod(h, 2)
            logits = scores[h] - cc_ref[p, pl.ds(start, tq), s:s + 1]
            if diagonal:
                logits = jnp.where(key <= qry, logits, NEG_BIG)
            m_new = jnp.maximum(carry[5 * p + 2 * s], jnp.max(logits, axis=0, keepdims=True))
            shifted.append((logits, m_new))
        stats = []
        for h in range(2 * pairs):
            p, s = divmod(h, 2)
            logits, m_new = shifted[h]
            alpha = jnp.exp(carry[5 * p + 2 * s] - m_new)
            prob = jnp.exp(logits - m_new)
            l = carry[5 * p + 2 * s + 1] * alpha + jnp.sum(prob, axis=0, keepdims=True)
            stats.append((m_new, l, alpha, prob.astype(BF16)))
        for p in range(pairs):
            sa, sb = stats[2 * p], stats[2 * p + 1]
            pv = _dot(vt_ref[p, j], jnp.concatenate([sa[3], sb[3]], axis=0))
            acc = carry[5 * p + 4] * jnp.where(first, sa[2], sb[2]) + pv
            out += [sa[0], sa[1], sb[0], sb[1], acc]
        return tuple(out)

    row_stat = (jnp.full((1, tq), NEG_BIG, F32), jnp.zeros((1, tq), F32))
    init = (row_stat * 2 + (jnp.zeros((LANES, tq), F32),)) * pairs
    state = block(i, init, True)
    res = lax.fori_loop(0, i, lambda j, c: block(j, c, False), state)
    slabs = []
    for p in range(pairs):
        inv = jnp.where(first, 1.0 / res[5 * p + 1], 1.0 / res[5 * p + 3])
        slabs.append((res[5 * p + 4] * inv).T)
    o = slabs[0] if pairs == 1 else jnp.concatenate(slabs, axis=1)
    o_ref[...] = o.astype(o_ref.dtype)


def _fox_attention(qkv, cum_cols, batch, seq, tq, n_pairs, q_col, k_col, v_col):
    T = qkv.shape[0]
    nq = seq // tq
    pairs = _pairs_per_step((FOX_PAIRS, 1), n_pairs, q_col, k_col, v_col)
    W = pairs * LANES
    qc, kc, vc = q_col // pairs, k_col // pairs, v_col // pairs
    return pl.pallas_call(
        functools.partial(_fox_kernel, pairs=pairs),
        grid=(batch, n_pairs // pairs, nq),
        in_specs=[
            pl.BlockSpec((tq, W), lambda b, p, i: (b * nq + i, qc + p)),
            pl.BlockSpec((seq, W), lambda b, p, i: (b, kc + p)),
            pl.BlockSpec((seq, W), lambda b, p, i: (b, vc + p)),
            pl.BlockSpec((pairs, seq, 2), lambda b, p, i: (p, b, 0)),
        ],
        out_specs=pl.BlockSpec((tq, W), lambda b, p, i: (b * nq + i, p)),
        out_shape=jax.ShapeDtypeStruct((T, n_pairs * LANES), BF16),
        scratch_shapes=[pltpu.VMEM((pairs, nq, LANES, 2 * tq), BF16)],
        compiler_params=_params(("arbitrary", "arbitrary", "arbitrary")),
        name="fox_attention",
    )(qkv, qkv, qkv, cum_cols)


def _conv_kernel(a_ref, g_ref, w_ref, cb_ref, lg_ref, lb_ref, o_ref, u_ref, s_ref, *, width, halo):
    ts = a_ref.shape[0]

    @pl.when(pl.program_id(1) == 0)
    def _():
        u_ref[0:halo, :] = jnp.zeros((halo, u_ref.shape[1]), F32)

    u_ref[halo:halo + ts, :] = a_ref[...] * _sigmoid(g_ref[...])
    span = ts + halo - SUBLANES
    for r in range(1, SUBLANES):
        s_ref[r - 1, 0:span, :] = u_ref[r:r + span, :]
    acc = jnp.zeros(a_ref.shape, F32) + cb_ref[...]
    for k in range(width):
        base, r = divmod(halo - (width - 1) + k, SUBLANES)
        lo = base * SUBLANES
        tap = u_ref[lo:lo + ts, :] if r == 0 else s_ref[r - 1, lo:lo + ts, :]
        acc = acc + w_ref[k:k + 1, :] * tap
    y = _ln_rows(acc) * lg_ref[...] + lb_ref[...]
    o_ref[...] = (y * _sigmoid(y)).astype(o_ref.dtype)
    u_ref[0:halo, :] = u_ref[ts:ts + halo, :]


def _conv_module(rest, conv_w, conv_b, ln_g, ln_b, batch, seq):
    T = rest.shape[0]
    width, C = conv_w.shape
    halo = -(-(width - 1) // SUBLANES) * SUBLANES
    ts = _pick(seq, (128,))
    ns = seq // ts
    kern = functools.partial(_conv_kernel, width=width, halo=halo)
    vec = lambda: pl.BlockSpec((1, C), lambda b, s: (0, 0))
    return pl.pallas_call(
        kern,
        grid=(batch, ns),
        in_specs=[
            pl.BlockSpec((ts, C), lambda b, s: (b * ns + s, 0)),
            pl.BlockSpec((ts, C), lambda b, s: (b * ns + s, 1)),
            pl.BlockSpec((width, C), lambda b, s: (0, 0)),
            vec(), vec(), vec(),
        ],
        out_specs=pl.BlockSpec((ts, C), lambda b, s: (b * ns + s, 0)),
        out_shape=jax.ShapeDtypeStruct((T, C), BF16),
        scratch_shapes=[pltpu.VMEM((ts + halo, C), F32),
                        pltpu.VMEM((SUBLANES - 1, ts + halo, C), F32)],
        compiler_params=_params(("arbitrary", "arbitrary")),
        name="conformer_conv",
    )(rest, rest, conv_w, conv_b.reshape(1, C), ln_g.reshape(1, C), ln_b.reshape(1, C))


def _outproj_kernel(osb_ref, ofx_ref, ocv_ref, w_ref, x_ref, gt_ref, g_ref, b_ref,
                    sc_ref, sh_ref, rwh_ref, rwl_ref, rb_ref,
                    x1_ref, h2_ref, lg_ref, *, alpha):
    d_sb = osb_ref.shape[1]
    d_fx = ofx_ref.shape[1]
    half = x_ref.shape[0] // 2
    ys = []
    for r in range(2):
        rows = slice(r * half, (r + 1) * half)
        y = _dot(osb_ref[rows, :], w_ref[0:d_sb, :])
        y = y + _dot(ofx_ref[rows, :], w_ref[d_sb:d_sb + d_fx, :])
        ys.append(y + _dot(ocv_ref[rows, :], w_ref[d_sb + d_fx:, :]))
    for r in range(2):
        rows = slice(r * half, (r + 1) * half)
        x1 = _ln_rows(alpha * x_ref[rows, :] + gt_ref[0] * ys[r]) * g_ref[...] + b_ref[...]
        x1_ref[rows, :] = x1
        h2 = _ln_rows(x1) * (1.0 + sc_ref[0]) + sh_ref[0]
        h2_ref[rows, :] = h2
        h_hi, h_lo = _split2(h2)
        lg = _dot(h_hi, rwh_ref[...]) + _dot(h_lo, rwh_ref[...]) + _dot(h_hi, rwl_ref[...])
        lg_ref[rows, :] = lg + rb_ref[...]


def _outproj(o_sb, o_fx, o_cv, w_out, x2d, gt, g, b, sc2, sh2, rw_hi, rw_lo, rb, seq, alpha):
    T, D = x2d.shape
    tm = _pick(seq, (512, 256, 128))
    per_b = seq // tm
    rowblk = lambda n: pl.BlockSpec((tm, n), lambda i: (i, 0))
    full = lambda r, c: pl.BlockSpec((r, c), lambda i: (0, 0))
    perb = lambda: pl.BlockSpec((1, 1, D), lambda i: (i // per_b, 0, 0))
    return pl.pallas_call(
        functools.partial(_outproj_kernel, alpha=alpha),
        grid=(T // tm,),
        in_specs=[
            rowblk(o_sb.shape[1]), rowblk(o_fx.shape[1]), rowblk(o_cv.shape[1]),
            full(D, D), rowblk(D), perb(), full(1, D), full(1, D), perb(), perb(),
            full(D, LANES), full(D, LANES), full(1, LANES),
        ],
        out_specs=[rowblk(D), rowblk(D), rowblk(LANES)],
        out_shape=[jax.ShapeDtypeStruct((T, D), F32), jax.ShapeDtypeStruct((T, D), F32),
                   jax.ShapeDtypeStruct((T, LANES), F32)],
        compiler_params=_params(("arbitrary",)),
        name="out_proj_norm_router",
    )(o_sb, o_fx, o_cv, w_out, x2d, gt, g, b, sc2, sh2, rw_hi, rw_lo, rb)


def _route_kernel(lg_ref, e_ref, w_ref, *, n_groups, epg):
    L = lg_ref[...]
    lane_i = lax.broadcasted_iota(jnp.int32, L.shape, 1)
    lane = lane_i.astype(F32)
    far = 1e6
    is_grp = lane < n_groups
    l1 = jnp.where(is_grp, L, NEG_BIG)
    m1 = jnp.max(l1, axis=1, keepdims=True)
    grp = jnp.min(jnp.where(is_grp & (l1 == m1), lane, far), axis=1, keepdims=True)
    s1 = jnp.sum(jnp.where(is_grp, jnp.exp(l1 - m1), 0.0), axis=1, keepdims=True)
    p_grp = 1.0 / s1
    lo = n_groups + grp * epg
    in_grp = (lane >= lo) & (lane < lo + epg)
    l2 = jnp.where(in_grp, L, NEG_BIG)
    m2 = jnp.max(l2, axis=1, keepdims=True)
    e2 = jnp.where(in_grp, jnp.exp(l2 - m2), -1.0)
    v1 = jnp.max(e2, axis=1, keepdims=True)
    i1 = jnp.min(jnp.where(e2 == v1, lane, far), axis=1, keepdims=True)
    e2b = jnp.where(lane == i1, -1.0, e2)
    v2 = jnp.max(e2b, axis=1, keepdims=True)
    i2 = jnp.min(jnp.where(e2b == v2, lane, far), axis=1, keepdims=True)
    den = v1 + v2
    w0 = p_grp * v1 / den
    w1 = p_grp * v2 / den
    ids = jnp.where(lane_i == 0, i1 - n_groups, jnp.where(lane_i == 1, i2 - n_groups, 0.0))
    e_ref[...] = ids.astype(jnp.int32)
    w_ref[...] = jnp.where(lane_i == 0, w0, jnp.where(lane_i == 1, w1, 0.0))


def _route(logits, n_groups, epg):
    T = logits.shape[0]
    tm = _pick(T, (512, 256, 128))
    blk = lambda: pl.BlockSpec((tm, LANES), lambda i: (i, 0))
    return pl.pallas_call(
        functools.partial(_route_kernel, n_groups=n_groups, epg=epg),
        grid=(T // tm,),
        in_specs=[blk()],
        out_specs=[blk(), blk()],
        out_shape=[jax.ShapeDtypeStruct((T, LANES), jnp.int32),
                   jax.ShapeDtypeStruct((T, LANES), F32)],
        compiler_params=_params(("arbitrary",)),
        name="route_topk",
    )(logits)


def _plan_kernel(e_ref, dest_ref, be_ref, cnt_ref, base_ref, *, n_experts):
    p = pl.program_id(0)
    i = pl.program_id(1)
    tm = e_ref.shape[0]
    lane = lax.broadcasted_iota(jnp.int32, (tm, LANES), 1)
    e = e_ref[...]
    hot0 = lane == e[:, 0:1]
    hot1 = lane == e[:, 1:2]
    both = jnp.where(hot0 | hot1, 1.0, 0.0)

    @pl.when((p == 0) & (i == 0))
    def _():
        cnt_ref[...] = jnp.zeros_like(cnt_ref)

    @pl.when(p == 0)
    def _():
        cnt_ref[...] += jnp.sum(both, axis=0, keepdims=True)

    @pl.when((p == 1) & (i == 0))
    def _():
        nblk = jnp.floor((cnt_ref[...] + (MOE_BLOCK - 1)) * (1.0 / MOE_BLOCK))
        r = lax.broadcasted_iota(jnp.int32, (LANES, LANES), 0)
        c = lax.broadcasted_iota(jnp.int32, (LANES, LANES), 1)
        before = jnp.where(r < c, 1.0, 0.0).astype(BF16)
        nb8 = jnp.broadcast_to(nblk, (SUBLANES, LANES)).astype(BF16)
        excl = _dot(nb8, before)[0:1, :]
        base_ref[...] = excl * MOE_BLOCK
        cnt_ref[...] = jnp.zeros_like(cnt_ref)
        incl = excl + nblk
        nb_rows = be_ref.shape[0]
        bidx = lax.broadcasted_iota(jnp.int32, (nb_rows, LANES), 0).astype(F32)
        lane_b = lax.broadcasted_iota(jnp.int32, (nb_rows, LANES), 1)
        done = jnp.where((incl <= bidx) & (lane_b < n_experts), 1.0, 0.0)
        be = jnp.sum(done, axis=1, keepdims=True)
        be_ref[...] = jnp.broadcast_to(be, (nb_rows, LANES)).astype(jnp.int32)

    @pl.when(p == 1)
    def _():
        r = lax.broadcasted_iota(jnp.int32, (tm, tm), 0)
        c = lax.broadcasted_iota(jnp.int32, (tm, tm), 1)
        earlier = jnp.where(c < r, 1.0, 0.0).astype(BF16)
        tot = _dot(earlier, both.astype(BF16)) + cnt_ref[...] + base_ref[...]
        d0 = jnp.sum(jnp.where(hot0, tot, 0.0), axis=1, keepdims=True)
        d1 = jnp.sum(jnp.where(hot1, tot, 0.0), axis=1, keepdims=True)
        dest = jnp.where(lane == 0, d0, jnp.where(lane == 1, d1, 0.0))
        dest_ref[...] = dest.astype(jnp.int32)
        cnt_ref[...] += jnp.sum(both, axis=0, keepdims=True)


def _plan(e_lanes, n_experts, n_blocks):
    T = e_lanes.shape[0]
    tm = _pick(T, (256, 128))
    nb_rows = -(-n_blocks // SUBLANES) * SUBLANES
    return pl.pallas_call(
        functools.partial(_plan_kernel, n_experts=n_experts),
        grid=(2, T // tm),
        in_specs=[pl.BlockSpec((tm, LANES), lambda p, i: (i, 0))],
        out_specs=[pl.BlockSpec((tm, LANES), lambda p, i: (i * p, 0)),
                   pl.BlockSpec((nb_rows, LANES), lambda p, i: (0, 0))],
        out_shape=[jax.ShapeDtypeStruct((T, LANES), jnp.int32),
                   jax.ShapeDtypeStruct((nb_rows, LANES), jnp.int32)],
        scratch_shapes=[pltpu.VMEM((1, LANES), F32), pltpu.VMEM((1, LANES), F32)],
        compiler_params=_params(("arbitrary", "arbitrary")),
        name="dispatch_plan",
    )(e_lanes)


def _row_copy(src, src_row, dst, dst_row, sem):
    return pltpu.make_async_copy(src.at[pl.ds(src_row, 1)], dst.at[pl.ds(dst_row, 1)], sem)


def _slotmap_kernel(dest_ref, tok_ref):
    def clear(s, carry):
        tok_ref[s] = 0
        return carry

    lax.fori_loop(0, tok_ref.shape[0], clear, 0, unroll=8)

    def put(t, carry):
        tok_ref[dest_ref[2 * t]] = t
        tok_ref[dest_ref[2 * t + 1]] = t
        return carry

    lax.fori_loop(0, dest_ref.shape[0] // 2, put, 0, unroll=8)


def _slot_map(dest_flat, n_slots):
    return pl.pallas_call(
        _slotmap_kernel,
        in_specs=[pl.BlockSpec(memory_space=pltpu.SMEM)],
        out_specs=pl.BlockSpec(memory_space=pltpu.SMEM),
        out_shape=jax.ShapeDtypeStruct((n_slots,), jnp.int32),
        name="moe_slot_map",
    )(dest_flat)


def _expert_kernel(be_ref, nu_ref, tok_ref, h_ref, wg_ref, wu_ref, wd_ref, o_ref, xbuf, sems):
    del be_ref
    b = pl.program_id(0)
    n_used = nu_ref[0]
    slot = b % 2

    def issue(blk, s, r):
        _row_copy(h_ref, tok_ref[blk * MOE_BLOCK + r], xbuf.at[s], r, sems.at[s]).start()

    @pl.when(b == 0)
    def _():
        def step(r, carry):
            issue(0, 0, r)
            return carry

        lax.fori_loop(0, MOE_BLOCK, step, 0, unroll=8)

    @pl.when(b + 1 < n_used)
    def _():
        for r in range(MOE_BLOCK):
            issue(b + 1, 1 - slot, r)

    @pl.when(b < n_used)
    def _():
        def drain(r, carry):
            _row_copy(h_ref, 0, xbuf.at[slot], 0, sems.at[slot]).wait()
            return carry

        lax.fori_loop(0, MOE_BLOCK, drain, 0, unroll=8)
        x = xbuf[slot].astype(BF16)
        g = _dot(x, wg_ref[0, 0])
        u = _dot(x, wu_ref[0, 0])
        hid = (g * _sigmoid(g)) * u
        o_ref[...] = _dot(hid.astype(BF16), wd_ref[0, 0])

    @pl.when(b >= n_used)
    def _():
        o_ref[...] = jnp.zeros_like(o_ref)


def _experts(block_expert, n_used, slot_tok, h2, w_gate, w_up, w_down, layer, n_blocks):
    D = h2.shape[1]
    DE = w_gate.shape[3]
    wmap = lambda b, be, nu, tok: (layer, be[jnp.minimum(b, nu[0] - 1)], 0, 0)
    return pl.pallas_call(
        _expert_kernel,
        grid_spec=pltpu.PrefetchScalarGridSpec(
            num_scalar_prefetch=3,
            grid=(n_blocks,),
            in_specs=[
                pl.BlockSpec(memory_space=pl.ANY),
                pl.BlockSpec((1, 1, D, DE), wmap),
                pl.BlockSpec((1, 1, D, DE), wmap),
                pl.BlockSpec((1, 1, DE, D), wmap),
            ],
            out_specs=pl.BlockSpec((MOE_BLOCK, D), lambda b, be, nu, tok: (b, 0)),
            scratch_shapes=[pltpu.VMEM((2, MOE_BLOCK, D), F32), pltpu.SemaphoreType.DMA((2,))],
        ),
        out_shape=jax.ShapeDtypeStruct((n_blocks * MOE_BLOCK, D), F32),
        compiler_params=_params(("arbitrary",)),
        name="moe_experts",
    )(block_expert, n_used, slot_tok, h2, w_gate, w_up, w_down)


def _combine_kernel(dest_ref, yb_ref, w_ref, x_ref, gt_ref, g_ref, b_ref, o_ref,
                    buf_ref, sems, *, alpha):
    tm = x_ref.shape[0]
    i = pl.program_id(0)
    slot = i % 2

    def issue(tile, s, r):
        t = tile * tm + r
        _row_copy(yb_ref, dest_ref[2 * t], buf_ref.at[s, 0], r, sems.at[s]).start()
        _row_copy(yb_ref, dest_ref[2 * t + 1], buf_ref.at[s, 1], r, sems.at[s]).start()

    @pl.when(i == 0)
    def _():
        def step(r, carry):
            issue(0, 0, r)
            return carry

        lax.fori_loop(0, tm, step, 0, unroll=8)

    @pl.when(i + 1 < pl.num_programs(0))
    def _():
        for r in range(tm):
            issue(i + 1, 1 - slot, r)

    def drain(r, carry):
        _row_copy(yb_ref, 0, buf_ref.at[slot, 0], 0, sems.at[slot]).wait()
        return carry

    lax.fori_loop(0, 2 * tm, drain, 0, unroll=8)
    w = w_ref[...]
    y = w[:, 0:1] * buf_ref[slot, 0] + w[:, 1:2] * buf_ref[slot, 1]
    v = alpha * x_ref[...] + gt_ref[0] * y
    o_ref[...] = _ln_rows(v) * g_ref[...] + b_ref[...]


def _combine(dest_flat, yb, w_lanes, x1, gt, g, b, seq, alpha):
    T, D = x1.shape
    tm = _pick(seq, (256, 128))
    per_b = seq // tm
    return pl.pallas_call(
        functools.partial(_combine_kernel, alpha=alpha),
        grid_spec=pltpu.PrefetchScalarGridSpec(
            num_scalar_prefetch=1,
            grid=(T // tm,),
            in_specs=[
                pl.BlockSpec(memory_space=pl.ANY),
                pl.BlockSpec((tm, LANES), lambda i, d: (i, 0)),
                pl.BlockSpec((tm, D), lambda i, d: (i, 0)),
                pl.BlockSpec((1, 1, D), lambda i, d: (i // per_b, 0, 0)),
                pl.BlockSpec((1, D), lambda i, d: (0, 0)),
                pl.BlockSpec((1, D), lambda i, d: (0, 0)),
            ],
            out_specs=pl.BlockSpec((tm, D), lambda i, d: (i, 0)),
            scratch_shapes=[pltpu.VMEM((2, 2, tm, D), F32), pltpu.SemaphoreType.DMA((2,))],
        ),
        out_shape=jax.ShapeDtypeStruct((T, D), F32),
        compiler_params=_params(("arbitrary",)),
        name="moe_combine_norm",
    )(dest_flat, yb, w_lanes, x1, gt, g, b)


def kernel(x, c, ada_w, ada_b, w_in, b_forget, conv_w, conv_b, conv_ln_g, conv_ln_b, w_out,
           ln1_g, ln1_b, r1_w, r1_b, r2_w, r2_b, w_gate, w_up, w_down, ln2_g, ln2_b):
    B, S, D = x.shape
    L = ada_w.shape[0]
    T = B * S
    alpha = float((2 * L) ** 0.25)
    d_sb, d_fx, c_cv = D // 4, D // 2, D // 4
    n_fx = d_fx // HEAD_DIM
    n_groups = r1_w.shape[-1]
    epg = r2_w.shape[-1]
    n_experts = n_groups * epg
    n_blocks = (2 * T) // MOE_BLOCK + n_experts
    n_slots = n_blocks * MOE_BLOCK
    qkv_cols = 3 * d_sb + 3 * d_fx
    tq = _pick(S, (512, 256, 128))

    rows = -(-B // SUBLANES) * SUBLANES
    c_pad = jnp.zeros((rows, D), F32).at[:B].set(c)
    mod_all = _ada(c_pad, ada_w, ada_b)

    wg16, wu16, wd16 = w_gate.astype(BF16), w_up.astype(BF16), w_down.astype(BF16)
    x2d = x.reshape(T, D)
    for l in range(L):
        mod = mod_all[l, :B]
        sh1, sc1, gt1, sh2, sc2, gt2 = [m.reshape(B, 1, D) for m in jnp.split(mod, 6, axis=-1)]

        h1 = _lnmod(x2d, sc1, sh1, S)
        w_qkv = w_in[l, :, :qkv_cols].astype(BF16)
        f_lo = qkv_cols
        g_lo = qkv_cols + n_fx
        w_rest = jnp.concatenate(
            [w_in[l, :, g_lo:g_lo + 2 * c_cv], w_in[l, :, f_lo:f_lo + n_fx],
             jnp.zeros((D, LANES - n_fx), F32)], axis=1).astype(BF16)
        qkv = _matmul(h1, w_qkv, BF16, "proj_qkv")
        rest = _matmul(h1, w_rest, F32, "proj_glu_forget")

        b_pad = jnp.zeros((1, LANES), F32).at[0, :n_fx].set(b_forget[l])
        cum = _forget_cumsum(rest, b_pad, B, S, (2 * c_cv) // LANES)[:, :n_fx]
        cum_cols = cum.reshape(T, n_fx // 2, 2).transpose(1, 0, 2)

        nb = LANES
        o_sb = _sb_attention(qkv, B, S, d_sb // nb, 0, d_sb // nb, 2 * d_sb // nb)
        fx0 = 3 * d_sb // nb
        o_fx = _fox_attention(qkv, cum_cols, B, S, tq, d_fx // nb, fx0, fx0 + d_fx // nb,
                              fx0 + 2 * d_fx // nb)
        o_cv = _conv_module(rest, conv_w[l], conv_b[l], conv_ln_g[l], conv_ln_b[l], B, S)

        rw = jnp.concatenate(
            [r1_w[l], r2_w[l].transpose(1, 0, 2).reshape(D, n_experts),
             jnp.zeros((D, LANES - n_groups - n_experts), F32)], axis=1)
        rw_hi = rw.astype(BF16)
        rw_lo = (rw - rw_hi.astype(F32)).astype(BF16)
        rb = jnp.zeros((1, LANES), F32).at[0, :n_groups].set(r1_b[l])
        rb = rb.at[0, n_groups:n_groups + n_experts].set(r2_b[l].reshape(-1))
        x1, h2, logits = _outproj(
            o_sb, o_fx, o_cv, w_out[l].astype(BF16), x2d, gt1,
            ln1_g[l].reshape(1, D), ln1_b[l].reshape(1, D), sc2, sh2, rw_hi, rw_lo, rb, S, alpha)

        e_lanes, w_lanes = _route(logits, n_groups, epg)
        dest_lanes, be_lanes = _plan(e_lanes, n_experts, n_blocks)
        dest_flat = dest_lanes[:, :2].reshape(-1)
        be = be_lanes[:n_blocks, 0]
        n_used = jnp.sum((be < n_experts).astype(jnp.int32)).reshape(1)
        block_expert = jnp.minimum(be, n_experts - 1)
        slot_tok = _slot_map(dest_flat, n_slots)
        yb = _experts(block_expert, n_used, slot_tok, h2, wg16, wu16, wd16, l, n_blocks)
        x2d = _combine(dest_flat, yb, w_lanes, x1, gt2,
                       ln2_g[l].reshape(1, D), ln2_b[l].reshape(1, D), S, alpha)
    return x2d.reshape(B, S, D)
```

```python
import functools

import jax
import jax.numpy as jnp
from jax import lax
from jax.experimental import pallas as pl
from jax.experimental.pallas import tpu as pltpu

LN_EPS = 1e-5
HEAD_DIM = 64
LANES = 128
SUBLANES = 8
MOE_BLOCK = 256
NEG_BIG = -1e30
SB_DEAD_LOG = -104.0
FOX_PAIRS = 4
VMEM_LIMIT = 56 * 1024 * 1024

F32 = jnp.float32
BF16 = jnp.bfloat16


def _pick(n, cands):
    for c in cands:
        if n % c == 0:
            return c
    return n


def _params(sem):
    return pltpu.CompilerParams(dimension_semantics=sem, vmem_limit_bytes=VMEM_LIMIT)


def _ln_rows(v):
    mu = jnp.mean(v, axis=-1, keepdims=True)
    d = v - mu
    var = jnp.mean(d * d, axis=-1, keepdims=True)
    return d * lax.rsqrt(var + LN_EPS)


def _log_sigmoid(z):
    return jnp.minimum(z, 0.0) - jnp.log1p(jnp.exp(-jnp.abs(z)))


def _sigmoid(z):
    return 1.0 / (1.0 + jnp.exp(-z))


def _split2(v):
    hi = v.astype(BF16)
    lo = (v - hi.astype(F32)).astype(BF16)
    return hi, lo


def _dot(a, b):
    return jnp.dot(a, b, preferred_element_type=F32)


def _dot_nt(a, b):
    return lax.dot_general(a, b, (((1,), (1,)), ((), ())), preferred_element_type=F32)


def _ada_kernel(c_ref, w_ref, b_ref, o_ref):
    c = c_ref[...]
    s = c * _sigmoid(c)
    s_hi, s_lo = _split2(s)
    w_hi, w_lo = _split2(w_ref[0])
    acc = _dot(s_hi, w_hi) + _dot(s_lo, w_hi) + _dot(s_hi, w_lo)
    o_ref[0] = acc + b_ref[0]


def _ada(c_pad, ada_w, ada_b):
    L, D, N = ada_w.shape
    rows = c_pad.shape[0]
    tn = _pick(N, (512, 256, 128))
    return pl.pallas_call(
        _ada_kernel,
        grid=(L, N // tn),
        in_specs=[
            pl.BlockSpec((rows, D), lambda l, n: (0, 0)),
            pl.BlockSpec((1, D, tn), lambda l, n: (l, 0, n)),
            pl.BlockSpec((1, 1, tn), lambda l, n: (l, 0, n)),
        ],
        out_specs=pl.BlockSpec((1, rows, tn), lambda l, n: (l, 0, n)),
        out_shape=jax.ShapeDtypeStruct((L, rows, N), F32),
        compiler_params=_params(("arbitrary", "arbitrary")),
        name="ada_mod",
    )(c_pad, ada_w, ada_b.reshape(L, 1, N))


def _lnmod_kernel(x_ref, sc_ref, sh_ref, o_ref):
    h = _ln_rows(x_ref[...]) * (1.0 + sc_ref[0]) + sh_ref[0]
    o_ref[...] = h.astype(o_ref.dtype)


def _lnmod(x2d, sc, sh, seq):
    T, D = x2d.shape
    tm = _pick(seq, (512, 256, 128))
    per_b = seq // tm
    return pl.pallas_call(
        _lnmod_kernel,
        grid=(T // tm,),
        in_specs=[
            pl.BlockSpec((tm, D), lambda i: (i, 0)),
            pl.BlockSpec((1, 1, D), lambda i: (i // per_b, 0, 0)),
            pl.BlockSpec((1, 1, D), lambda i: (i // per_b, 0, 0)),
        ],
        out_specs=pl.BlockSpec((tm, D), lambda i: (i, 0)),
        out_shape=jax.ShapeDtypeStruct((T, D), BF16),
        compiler_params=_params(("arbitrary",)),
        name="ln_mod",
    )(x2d, sc, sh)


def _mm_kernel(a_ref, b_ref, o_ref):
    o_ref[...] = _dot(a_ref[...], b_ref[...]).astype(o_ref.dtype)


def _matmul(a, w, out_dtype, name):
    M, K = a.shape
    N = w.shape[1]
    tm = _pick(M, (1024, 512, 256, 128))
    tn = _pick(N, (512, 384, 256, 128))
    return pl.pallas_call(
        _mm_kernel,
        grid=(M // tm, N // tn),
        in_specs=[
            pl.BlockSpec((tm, K), lambda i, j: (i, 0)),
            pl.BlockSpec((K, tn), lambda i, j: (0, j)),
        ],
        out_specs=pl.BlockSpec((tm, tn), lambda i, j: (i, j)),
        out_shape=jax.ShapeDtypeStruct((M, N), out_dtype),
        compiler_params=_params(("arbitrary", "arbitrary")),
        name=name,
    )(a, w)


def _cum_kernel(f_ref, b_ref, o_ref, carry_ref):
    @pl.when(pl.program_id(1) == 0)
    def _():
        carry_ref[...] = jnp.zeros_like(carry_ref)

    ts = f_ref.shape[0]
    lf = _log_sigmoid(f_ref[...] + b_ref[...])
    p1 = lf.astype(BF16)
    r1 = lf - p1.astype(F32)
    p2 = r1.astype(BF16)
    p3 = (r1 - p2.astype(F32)).astype(BF16)
    row = lax.broadcasted_iota(jnp.int32, (ts, ts), 0)
    col = lax.broadcasted_iota(jnp.int32, (ts, ts), 1)
    tri = jnp.where(col <= row, 1.0, 0.0).astype(BF16)
    cum = _dot(tri, p1) + _dot(tri, p2) + _dot(tri, p3) + carry_ref[...]
    o_ref[...] = cum
    carry_ref[...] = cum[ts - 1:ts, :]


def _forget_cumsum(rest, b_pad, batch, seq, col_block):
    T = rest.shape[0]
    ts = _pick(seq, (256, 128))
    ns = seq // ts
    return pl.pallas_call(
        _cum_kernel,
        grid=(batch, ns),
        in_specs=[
            pl.BlockSpec((ts, LANES), lambda b, s: (b * ns + s, col_block)),
            pl.BlockSpec((1, LANES), lambda b, s: (0, 0)),
        ],
        out_specs=pl.BlockSpec((ts, LANES), lambda b, s: (b * ns + s, 0)),
        out_shape=jax.ShapeDtypeStruct((T, LANES), F32),
        scratch_shapes=[pltpu.VMEM((1, LANES), F32)],
        compiler_params=_params(("arbitrary", "arbitrary")),
        name="forget_cumsum",
    )(rest, b_pad)


def _masked_heads(q_ref, pairs):
    lane = lax.broadcasted_iota(jnp.int32, (1, LANES), 1)
    heads = []
    for p in range(pairs):
        q2 = q_ref[:, p * LANES:(p + 1) * LANES] * jnp.asarray(HEAD_DIM ** -0.5, q_ref.dtype)
        zero = jnp.zeros_like(q2)
        heads += [jnp.where(lane < HEAD_DIM, q2, zero), jnp.where(lane >= HEAD_DIM, q2, zero)]
    return heads


def _pairs_per_step(cands, *counts):
    return next(c for c in cands if all(n % c == 0 for n in counts))


def _store_value_transposed(v_ref, vt_ref, first):
    pairs, nk, _, two_tk = vt_ref.shape
    tk = two_tk // 2
    for p in range(pairs):
        for j in range(nk):
            vt = v_ref[j * tk:(j + 1) * tk, p * LANES:(p + 1) * LANES].astype(F32).T
            both = jnp.concatenate([jnp.where(first, vt, 0.0), jnp.where(first, 0.0, vt)], axis=1)
            vt_ref[p, j] = both.astype(BF16)


def _sb_kernel(q_ref, k_ref, v_ref, o_ref, vt_ref, *, pairs):
    tq = q_ref.shape[0]
    i = pl.program_id(2)
    q_heads = _masked_heads(q_ref, pairs)
    first = lax.broadcasted_iota(jnp.int32, (LANES, 1), 0) < HEAD_DIM

    @pl.when(i == 0)
    def _():
        for p in range(pairs):
            for j in range(vt_ref.shape[1]):
                vt = v_ref[j * tq:(j + 1) * tq, p * LANES:(p + 1) * LANES].astype(F32).T
                vt_ref[p, j] = vt.astype(BF16)

    q_pair = [jnp.concatenate([q_heads[2 * p], q_heads[2 * p + 1]], axis=0) for p in range(pairs)]
    key = lax.broadcasted_iota(jnp.int32, (tq, tq), 0)
    qry = lax.broadcasted_iota(jnp.int32, (tq, tq), 1)
    later_keys = jnp.where(qry > key, 1.0, 0.0).astype(BF16)
    causal = jnp.concatenate([key < qry, key < qry], axis=1)

    def block(j, state, diagonal):
        start = pl.multiple_of(j * tq, tq)
        out = []
        scores = [_dot_nt(k_ref[pl.ds(start, tq), p * LANES:(p + 1) * LANES], q_pair[p])
                  for p in range(pairs)]
        log_sig, log_keep = [], []
        for z in scores:
            ls = jnp.minimum(z, 0.0) - jnp.log(1.0 + jnp.exp(-jnp.abs(z)))
            lk = ls - z
            if diagonal:
                lk = jnp.where(causal, lk, 0.0)
            log_sig.append(ls)
            log_keep.append(lk)
        later = []
        for lk in log_keep:
            lk_hi, lk_lo = _split2(lk)
            later.append(_dot(later_keys, lk_hi) + _dot(later_keys, lk_lo))
        weights = []
        for p in range(pairs):
            w = jnp.exp(log_sig[p] + later[p] + state[2 * p])
            if diagonal:
                w = jnp.where(causal, w, 0.0)
            weights.append(w.astype(BF16))
        for p in range(pairs):
            pv = _dot(vt_ref[p, j], weights[p])
            acc = state[2 * p + 1] + jnp.where(first, pv[:, :tq], pv[:, tq:])
            out += [state[2 * p] + jnp.sum(log_keep[p], axis=0, keepdims=True), acc]
        return tuple(out)

    def live(state):
        top = state[0]
        for p in range(1, pairs):
            top = jnp.maximum(top, state[2 * p])
        return (jnp.max(top) > SB_DEAD_LOG).astype(jnp.int32)

    init = (jnp.zeros((1, 2 * tq), F32), jnp.zeros((LANES, tq), F32)) * pairs
    state = block(i, init, True)

    def cond(carry):
        return (carry[0] >= 0) & (carry[1] > 0)

    def body(carry):
        state = block(carry[0], carry[2:], False)
        return (carry[0] - 1, live(state)) + state

    res = lax.while_loop(cond, body, (i - 1, live(state)) + state)
    slabs = [res[2 + 2 * p + 1].T for p in range(pairs)]
    o = slabs[0] if pairs == 1 else jnp.concatenate(slabs, axis=1)
    o_ref[...] = o.astype(o_ref.dtype)


def _sb_attention(qkv, batch, seq, n_pairs, q_col, k_col, v_col):
    T = qkv.shape[0]
    tq = _pick(seq, (256, 128))
    nq = seq // tq
    pairs = _pairs_per_step((4, 2, 1), n_pairs, q_col, k_col, v_col)
    W = pairs * LANES
    qc, kc, vc = q_col // pairs, k_col // pairs, v_col // pairs
    return pl.pallas_call(
        functools.partial(_sb_kernel, pairs=pairs),
        grid=(batch, n_pairs // pairs, nq),
        in_specs=[
            pl.BlockSpec((tq, W), lambda b, p, i: (b * nq + i, qc + p)),
            pl.BlockSpec((seq, W), lambda b, p, i: (b, kc + p)),
            pl.BlockSpec((seq, W), lambda b, p, i: (b, vc + p)),
        ],
        out_specs=pl.BlockSpec((tq, W), lambda b, p, i: (b * nq + i, p)),
        out_shape=jax.ShapeDtypeStruct((T, n_pairs * LANES), BF16),
        scratch_shapes=[pltpu.VMEM((pairs, nq, LANES, tq), BF16)],
        compiler_params=_params(("arbitrary", "arbitrary", "arbitrary")),
        name="sb_attention",
    )(qkv, qkv, qkv)


def _fox_kernel(q_ref, k_ref, v_ref, cc_ref, o_ref, vt_ref, *, pairs):
    tq = q_ref.shape[0]
    i = pl.program_id(2)
    q_heads = _masked_heads(q_ref, pairs)
    first = lax.broadcasted_iota(jnp.int32, (LANES, 1), 0) < HEAD_DIM

    @pl.when(i == 0)
    def _():
        _store_value_transposed(v_ref, vt_ref, first)

    key = lax.broadcasted_iota(jnp.int32, (tq, tq), 0)
    qry = lax.broadcasted_iota(jnp.int32, (tq, tq), 1)

    def block(j, carry, diagonal):
        start = pl.multiple_of(j * tq, tq)
        out = []
        scores = [_dot_nt(k_ref[pl.ds(start, tq), (h // 2) * LANES:(h // 2 + 1) * LANES], q_heads[h])
                  for h in range(2 * pairs)]
        shifted = []
        for h in range(2 * pairs):
            p, s = divmod(h, 2)
            logits = scores[h] - cc_ref[0, pl.ds(start, tq), h:h + 1]
            if diagonal:
                logits = jnp.where(key <= qry, logits, NEG_BIG)
            m_new = jnp.maximum(carry[5 * p + 2 * s], jnp.max(logits, axis=0, keepdims=True))
            shifted.append((logits, m_new))
        stats = []
        for h in range(2 * pairs):
            p, s = divmod(h, 2)
            logits, m_new = shifted[h]
            alpha = jnp.exp(carry[5 * p + 2 * s] - m_new)
            prob = jnp.exp(logits - m_new)
            l = carry[5 * p + 2 * s + 1] * alpha + jnp.sum(prob, axis=0, keepdims=True)
            stats.append((m_new, l, alpha, prob.astype(BF16)))
        for p in range(pairs):
            sa, sb = stats[2 * p], stats[2 * p + 1]
            pv = _dot(vt_ref[p, j], jnp.concatenate([sa[3], sb[3]], axis=0))
            acc = carry[5 * p + 4] * jnp.where(first, sa[2], sb[2]) + pv
            out += [sa[0], sa[1], sb[0], sb[1], acc]
        return tuple(out)

    row_stat = (jnp.full((1, tq), NEG_BIG, F32), jnp.zeros((1, tq), F32))
    init = (row_stat * 2 + (jnp.zeros((LANES, tq), F32),)) * pairs
    state = block(i, init, True)
    res = lax.fori_loop(0, i, lambda j, c: block(j, c, False), state)
    slabs = []
    for p in range(pairs):
        inv = jnp.where(first, 1.0 / res[5 * p + 1], 1.0 / res[5 * p + 3])
        slabs.append((res[5 * p + 4] * inv).T)
    o = slabs[0] if pairs == 1 else jnp.concatenate(slabs, axis=1)
    o_ref[...] = o.astype(o_ref.dtype)


def _fox_attention(qkv, cum, batch, seq, tq, n_pairs, q_col, k_col, v_col):
    T = qkv.shape[0]
    nq = seq // tq
    pairs = _pairs_per_step((FOX_PAIRS, 2, 1), n_pairs, q_col, k_col, v_col)
    W = pairs * LANES
    qc, kc, vc = q_col // pairs, k_col // pairs, v_col // pairs
    return pl.pallas_call(
        functools.partial(_fox_kernel, pairs=pairs),
        grid=(batch, n_pairs // pairs, nq),
        in_specs=[
            pl.BlockSpec((tq, W), lambda b, p, i: (b * nq + i, qc + p)),
            pl.BlockSpec((seq, W), lambda b, p, i: (b, kc + p)),
            pl.BlockSpec((seq, W), lambda b, p, i: (b, vc + p)),
            pl.BlockSpec((1, seq, 2 * pairs), lambda b, p, i: (p, b, 0)),
        ],
        out_specs=pl.BlockSpec((tq, W), lambda b, p, i: (b * nq + i, p)),
        out_shape=jax.ShapeDtypeStruct((T, n_pairs * LANES), BF16),
        scratch_shapes=[pltpu.VMEM((pairs, nq, LANES, 2 * tq), BF16)],
        compiler_params=_params(("arbitrary", "arbitrary", "arbitrary")),
        name="fox_attention",
    )(qkv, qkv, qkv, cum.reshape(T, n_pairs // pairs, 2 * pairs).transpose(1, 0, 2))


def _conv_kernel(a_ref, g_ref, w_ref, cb_ref, lg_ref, lb_ref, o_ref, u_ref, s_ref, *, width, halo):
    ts = a_ref.shape[0]

    @pl.when(pl.program_id(1) == 0)
    def _():
        u_ref[0:halo, :] = jnp.zeros((halo, u_ref.shape[1]), F32)

    u_ref[halo:halo + ts, :] = a_ref[...] * _sigmoid(g_ref[...])
    span = ts + halo - SUBLANES
    for r in range(1, SUBLANES):
        s_ref[r - 1, 0:span, :] = u_ref[r:r + span, :]
    acc = jnp.zeros(a_ref.shape, F32) + cb_ref[...]
    for k in range(width):
        base, r = divmod(halo - (width - 1) + k, SUBLANES)
        lo = base * SUBLANES
        tap = u_ref[lo:lo + ts, :] if r == 0 else s_ref[r - 1, lo:lo + ts, :]
        acc = acc + w_ref[k:k + 1, :] * tap
    y = _ln_rows(acc) * lg_ref[...] + lb_ref[...]
    o_ref[...] = (y * _sigmoid(y)).astype(o_ref.dtype)
    u_ref[0:halo, :] = u_ref[ts:ts + halo, :]


def _conv_module(rest, conv_w, conv_b, ln_g, ln_b, batch, seq):
    T = rest.shape[0]
    width, C = conv_w.shape
    halo = -(-(width - 1) // SUBLANES) * SUBLANES
    ts = _pick(seq, (128,))
    ns = seq // ts
    kern = functools.partial(_conv_kernel, width=width, halo=halo)
    vec = lambda: pl.BlockSpec((1, C), lambda b, s: (0, 0))
    return pl.pallas_call(
        kern,
        grid=(batch, ns),
        in_specs=[
            pl.BlockSpec((ts, C), lambda b, s: (b * ns + s, 0)),
            pl.BlockSpec((ts, C), lambda b, s: (b * ns + s, 1)),
            pl.BlockSpec((width, C), lambda b, s: (0, 0)),
            vec(), vec(), vec(),
        ],
        out_specs=pl.BlockSpec((ts, C), lambda b, s: (b * ns + s, 0)),
        out_shape=jax.ShapeDtypeStruct((T, C), BF16),
        scratch_shapes=[pltpu.VMEM((ts + halo, C), F32),
                        pltpu.VMEM((SUBLANES - 1, ts + halo, C), F32)],
        compiler_params=_params(("arbitrary", "arbitrary")),
        name="conformer_conv",
    )(rest, rest, conv_w, conv_b.reshape(1, C), ln_g.reshape(1, C), ln_b.reshape(1, C))


def _outproj_kernel(osb_ref, ofx_ref, ocv_ref, w_ref, x_ref, gt_ref, g_ref, b_ref,
                    sc_ref, sh_ref, rwh_ref, rwl_ref, rb_ref,
                    x1_ref, h2_ref, lg_ref, *, alpha):
    d_sb = osb_ref.shape[1]
    d_fx = ofx_ref.shape[1]
    half = x_ref.shape[0] // 2
    ys = []
    for r in range(2):
        rows = slice(r * half, (r + 1) * half)
        y = _dot(osb_ref[rows, :], w_ref[0:d_sb, :])
        y = y + _dot(ofx_ref[rows, :], w_ref[d_sb:d_sb + d_fx, :])
        ys.append(y + _dot(ocv_ref[rows, :], w_ref[d_sb + d_fx:, :]))
    for r in range(2):
        rows = slice(r * half, (r + 1) * half)
        x1 = _ln_rows(alpha * x_ref[rows, :] + gt_ref[0] * ys[r]) * g_ref[...] + b_ref[...]
        x1_ref[rows, :] = x1
        h2 = _ln_rows(x1) * (1.0 + sc_ref[0]) + sh_ref[0]
        h2_ref[rows, :] = h2
        h_hi, h_lo = _split2(h2)
        lg = _dot(h_hi, rwh_ref[...]) + _dot(h_lo, rwh_ref[...]) + _dot(h_hi, rwl_ref[...])
        lg_ref[rows, :] = lg + rb_ref[...]


def _outproj(o_sb, o_fx, o_cv, w_out, x2d, gt, g, b, sc2, sh2, rw_hi, rw_lo, rb, seq, alpha):
    T, D = x2d.shape
    tm = _pick(seq, (512, 256, 128))
    per_b = seq // tm
    rowblk = lambda n: pl.BlockSpec((tm, n), lambda i: (i, 0))
    full = lambda r, c: pl.BlockSpec((r, c), lambda i: (0, 0))
    perb = lambda: pl.BlockSpec((1, 1, D), lambda i: (i // per_b, 0, 0))
    return pl.pallas_call(
        functools.partial(_outproj_kernel, alpha=alpha),
        grid=(T // tm,),
        in_specs=[
            rowblk(o_sb.shape[1]), rowblk(o_fx.shape[1]), rowblk(o_cv.shape[1]),
            full(D, D), rowblk(D), perb(), full(1, D), full(1, D), perb(), perb(),
            full(D, LANES), full(D, LANES), full(1, LANES),
        ],
        out_specs=[rowblk(D), rowblk(D), rowblk(LANES)],
        out_shape=[jax.ShapeDtypeStruct((T, D), F32), jax.ShapeDtypeStruct((T, D), F32),
                   jax.ShapeDtypeStruct((T, LANES), F32)],
        compiler_params=_params(("arbitrary",)),
        name="out_proj_norm_router",
    )(o_sb, o_fx, o_cv, w_out, x2d, gt, g, b, sc2, sh2, rw_hi, rw_lo, rb)


def _route_kernel(lg_ref, e_ref, w_ref, *, n_groups, epg):
    L = lg_ref[...]
    lane_i = lax.broadcasted_iota(jnp.int32, L.shape, 1)
    lane = lane_i.astype(F32)
    far = 1e6
    is_grp = lane < n_groups
    l1 = jnp.where(is_grp, L, NEG_BIG)
    m1 = jnp.max(l1, axis=1, keepdims=True)
    grp = jnp.min(jnp.where(is_grp & (l1 == m1), lane, far), axis=1, keepdims=True)
    s1 = jnp.sum(jnp.where(is_grp, jnp.exp(l1 - m1), 0.0), axis=1, keepdims=True)
    p_grp = 1.0 / s1
    lo = n_groups + grp * epg
    in_grp = (lane >= lo) & (lane < lo + epg)
    l2 = jnp.where(in_grp, L, NEG_BIG)
    m2 = jnp.max(l2, axis=1, keepdims=True)
    e2 = jnp.where(in_grp, jnp.exp(l2 - m2), -1.0)
    v1 = jnp.max(e2, axis=1, keepdims=True)
    i1 = jnp.min(jnp.where(e2 == v1, lane, far), axis=1, keepdims=True)
    e2b = jnp.where(lane == i1, -1.0, e2)
    v2 = jnp.max(e2b, axis=1, keepdims=True)
    i2 = jnp.min(jnp.where(e2b == v2, lane, far), axis=1, keepdims=True)
    den = v1 + v2
    w0 = p_grp * v1 / den
    w1 = p_grp * v2 / den
    ids = jnp.where(lane_i == 0, i1 - n_groups, jnp.where(lane_i == 1, i2 - n_groups, 0.0))
    e_ref[...] = ids.astype(jnp.int32)
    w_ref[...] = jnp.where(lane_i == 0, w0, jnp.where(lane_i == 1, w1, 0.0))


def _route(logits, n_groups, epg):
    T = logits.shape[0]
    tm = _pick(T, (512, 256, 128))
    blk = lambda: pl.BlockSpec((tm, LANES), lambda i: (i, 0))
    return pl.pallas_call(
        functools.partial(_route_kernel, n_groups=n_groups, epg=epg),
        grid=(T // tm,),
        in_specs=[blk()],
        out_specs=[blk(), blk()],
        out_shape=[jax.ShapeDtypeStruct((T, LANES), jnp.int32),
                   jax.ShapeDtypeStruct((T, LANES), F32)],
        compiler_params=_params(("arbitrary",)),
        name="route_topk",
    )(logits)


def _plan_kernel(e_ref, dest_ref, be_ref, cnt_ref, base_ref, *, n_experts):
    p = pl.program_id(0)
    i = pl.program_id(1)
    tm = e_ref.shape[0]
    lane = lax.broadcasted_iota(jnp.int32, (tm, LANES), 1)
    e = e_ref[...]
    hot0 = lane == e[:, 0:1]
    hot1 = lane == e[:, 1:2]
    both = jnp.where(hot0 | hot1, 1.0, 0.0)

    @pl.when((p == 0) & (i == 0))
    def _():
        cnt_ref[...] = jnp.zeros_like(cnt_ref)

    @pl.when(p == 0)
    def _():
        cnt_ref[...] += jnp.sum(both, axis=0, keepdims=True)

    @pl.when((p == 1) & (i == 0))
    def _():
        nblk = jnp.floor((cnt_ref[...] + (MOE_BLOCK - 1)) * (1.0 / MOE_BLOCK))
        r = lax.broadcasted_iota(jnp.int32, (LANES, LANES), 0)
        c = lax.broadcasted_iota(jnp.int32, (LANES, LANES), 1)
        before = jnp.where(r < c, 1.0, 0.0).astype(BF16)
        nb8 = jnp.broadcast_to(nblk, (SUBLANES, LANES)).astype(BF16)
        excl = _dot(nb8, before)[0:1, :]
        base_ref[...] = excl * MOE_BLOCK
        cnt_ref[...] = jnp.zeros_like(cnt_ref)
        incl = excl + nblk
        nb_rows = be_ref.shape[0]
        bidx = lax.broadcasted_iota(jnp.int32, (nb_rows, LANES), 0).astype(F32)
        lane_b = lax.broadcasted_iota(jnp.int32, (nb_rows, LANES), 1)
        done = jnp.where((incl <= bidx) & (lane_b < n_experts), 1.0, 0.0)
        be = jnp.sum(done, axis=1, keepdims=True)
        be_ref[...] = jnp.broadcast_to(be, (nb_rows, LANES)).astype(jnp.int32)

    @pl.when(p == 1)
    def _():
        r = lax.broadcasted_iota(jnp.int32, (tm, tm), 0)
        c = lax.broadcasted_iota(jnp.int32, (tm, tm), 1)
        earlier = jnp.where(c < r, 1.0, 0.0).astype(BF16)
        tot = _dot(earlier, both.astype(BF16)) + cnt_ref[...] + base_ref[...]
        d0 = jnp.sum(jnp.where(hot0, tot, 0.0), axis=1, keepdims=True)
        d1 = jnp.sum(jnp.where(hot1, tot, 0.0), axis=1, keepdims=True)
        dest = jnp.where(lane == 0, d0, jnp.where(lane == 1, d1, 0.0))
        dest_ref[...] = dest.astype(jnp.int32)
        cnt_ref[...] += jnp.sum(both, axis=0, keepdims=True)


def _plan(e_lanes, n_experts, n_blocks):
    T = e_lanes.shape[0]
    tm = _pick(T, (256, 128))
    nb_rows = -(-n_blocks // SUBLANES) * SUBLANES
    return pl.pallas_call(
        functools.partial(_plan_kernel, n_experts=n_experts),
        grid=(2, T // tm),
        in_specs=[pl.BlockSpec((tm, LANES), lambda p, i: (i, 0))],
        out_specs=[pl.BlockSpec((tm, LANES), lambda p, i: (i * p, 0)),
                   pl.BlockSpec((nb_rows, LANES), lambda p, i: (0, 0))],
        out_shape=[jax.ShapeDtypeStruct((T, LANES), jnp.int32),
                   jax.ShapeDtypeStruct((nb_rows, LANES), jnp.int32)],
        scratch_shapes=[pltpu.VMEM((1, LANES), F32), pltpu.VMEM((1, LANES), F32)],
        compiler_params=_params(("arbitrary", "arbitrary")),
        name="dispatch_plan",
    )(e_lanes)


def _row_copy(src, src_row, dst, dst_row, sem):
    return pltpu.make_async_copy(src.at[pl.ds(src_row, 1)], dst.at[pl.ds(dst_row, 1)], sem)


def _slotmap_kernel(dest_ref, tok_ref):
    def clear(s, carry):
        tok_ref[s] = 0
        return carry

    lax.fori_loop(0, tok_ref.shape[0], clear, 0, unroll=8)

    def put(t, carry):
        tok_ref[dest_ref[2 * t]] = t
        tok_ref[dest_ref[2 * t + 1]] = t
        return carry

    lax.fori_loop(0, dest_ref.shape[0] // 2, put, 0, unroll=8)


def _slot_map(dest_flat, n_slots):
    return pl.pallas_call(
        _slotmap_kernel,
        in_specs=[pl.BlockSpec(memory_space=pltpu.SMEM)],
        out_specs=pl.BlockSpec(memory_space=pltpu.SMEM),
        out_shape=jax.ShapeDtypeStruct((n_slots,), jnp.int32),
        name="moe_slot_map",
    )(dest_flat)


def _expert_kernel(be_ref, nu_ref, tok_ref, h_ref, wg_ref, wu_ref, wd_ref, o_ref, xbuf, sems):
    del be_ref
    b = pl.program_id(0)
    n_used = nu_ref[0]
    slot = b % 2

    def issue(blk, s, r):
        _row_copy(h_ref, tok_ref[blk * MOE_BLOCK + r], xbuf.at[s], r, sems.at[s]).start()

    @pl.when(b == 0)
    def _():
        def step(r, carry):
            issue(0, 0, r)
            return carry

        lax.fori_loop(0, MOE_BLOCK, step, 0, unroll=8)

    @pl.when(b + 1 < n_used)
    def _():
        for r in range(MOE_BLOCK):
            issue(b + 1, 1 - slot, r)

    @pl.when(b < n_used)
    def _():
        def drain(r, carry):
            _row_copy(h_ref, 0, xbuf.at[slot], 0, sems.at[slot]).wait()
            return carry

        lax.fori_loop(0, MOE_BLOCK, drain, 0, unroll=8)
        x = xbuf[slot].astype(BF16)
        g = _dot(x, wg_ref[0, 0])
        u = _dot(x, wu_ref[0, 0])
        hid = (g * _sigmoid(g)) * u
        o_ref[...] = _dot(hid.astype(BF16), wd_ref[0, 0])

    @pl.when(b >= n_used)
    def _():
        o_ref[...] = jnp.zeros_like(o_ref)


def _experts(block_expert, n_used, slot_tok, h2, w_gate, w_up, w_down, layer, n_blocks):
    D = h2.shape[1]
    DE = w_gate.shape[3]
    wmap = lambda b, be, nu, tok: (layer, be[jnp.minimum(b, nu[0] - 1)], 0, 0)
    return pl.pallas_call(
        _expert_kernel,
        grid_spec=pltpu.PrefetchScalarGridSpec(
            num_scalar_prefetch=3,
            grid=(n_blocks,),
            in_specs=[
                pl.BlockSpec(memory_space=pl.ANY),
                pl.BlockSpec((1, 1, D, DE), wmap),
                pl.BlockSpec((1, 1, D, DE), wmap),
                pl.BlockSpec((1, 1, DE, D), wmap),
            ],
            out_specs=pl.BlockSpec((MOE_BLOCK, D), lambda b, be, nu, tok: (b, 0)),
            scratch_shapes=[pltpu.VMEM((2, MOE_BLOCK, D), F32), pltpu.SemaphoreType.DMA((2,))],
        ),
        out_shape=jax.ShapeDtypeStruct((n_blocks * MOE_BLOCK, D), F32),
        compiler_params=_params(("arbitrary",)),
        name="moe_experts",
    )(block_expert, n_used, slot_tok, h2, w_gate, w_up, w_down)


def _combine_kernel(dest_ref, yb_ref, w_ref, x_ref, gt_ref, g_ref, b_ref, o_ref,
                    buf_ref, sems, *, alpha):
    tm = x_ref.shape[0]
    i = pl.program_id(0)
    slot = i % 2

    def issue(tile, s, r):
        t = tile * tm + r
        _row_copy(yb_ref, dest_ref[2 * t], buf_ref.at[s, 0], r, sems.at[s]).start()
        _row_copy(yb_ref, dest_ref[2 * t + 1], buf_ref.at[s, 1], r, sems.at[s]).start()

    @pl.when(i == 0)
    def _():
        def step(r, carry):
            issue(0, 0, r)
            return carry

        lax.fori_loop(0, tm, step, 0, unroll=8)

    @pl.when(i + 1 < pl.num_programs(0))
    def _():
        for r in range(tm):
            issue(i + 1, 1 - slot, r)

    def drain(r, carry):
        _row_copy(yb_ref, 0, buf_ref.at[slot, 0], 0, sems.at[slot]).wait()
        return carry

    lax.fori_loop(0, 2 * tm, drain, 0, unroll=8)
    w = w_ref[...]
    y = w[:, 0:1] * buf_ref[slot, 0] + w[:, 1:2] * buf_ref[slot, 1]
    v = alpha * x_ref[...] + gt_ref[0] * y
    o_ref[...] = _ln_rows(v) * g_ref[...] + b_ref[...]


def _combine(dest_flat, yb, w_lanes, x1, gt, g, b, seq, alpha):
    T, D = x1.shape
    tm = _pick(seq, (256, 128))
    per_b = seq // tm
    return pl.pallas_call(
        functools.partial(_combine_kernel, alpha=alpha),
        grid_spec=pltpu.PrefetchScalarGridSpec(
            num_scalar_prefetch=1,
            grid=(T // tm,),
            in_specs=[
                pl.BlockSpec(memory_space=pl.ANY),
                pl.BlockSpec((tm, LANES), lambda i, d: (i, 0)),
                pl.BlockSpec((tm, D), lambda i, d: (i, 0)),
                pl.BlockSpec((1, 1, D), lambda i, d: (i // per_b, 0, 0)),
                pl.BlockSpec((1, D), lambda i, d: (0, 0)),
                pl.BlockSpec((1, D), lambda i, d: (0, 0)),
            ],
            out_specs=pl.BlockSpec((tm, D), lambda i, d: (i, 0)),
            scratch_shapes=[pltpu.VMEM((2, 2, tm, D), F32), pltpu.SemaphoreType.DMA((2,))],
        ),
        out_shape=jax.ShapeDtypeStruct((T, D), F32),
        compiler_params=_params(("arbitrary",)),
        name="moe_combine_norm",
    )(dest_flat, yb, w_lanes, x1, gt, g, b)


def kernel(x, c, ada_w, ada_b, w_in, b_forget, conv_w, conv_b, conv_ln_g, conv_ln_b, w_out,
           ln1_g, ln1_b, r1_w, r1_b, r2_w, r2_b, w_gate, w_up, w_down, ln2_g, ln2_b):
    B, S, D = x.shape
    L = ada_w.shape[0]
    T = B * S
    alpha = float((2 * L) ** 0.25)
    d_sb, d_fx, c_cv = D // 4, D // 2, D // 4
    n_fx = d_fx // HEAD_DIM
    n_groups = r1_w.shape[-1]
    epg = r2_w.shape[-1]
    n_experts = n_groups * epg
    n_blocks = (2 * T) // MOE_BLOCK + n_experts
    n_slots = n_blocks * MOE_BLOCK
    qkv_cols = 3 * d_sb + 3 * d_fx
    tq = _pick(S, (512, 256, 128))

    rows = -(-B // SUBLANES) * SUBLANES
    c_pad = jnp.zeros((rows, D), F32).at[:B].set(c)
    mod_all = _ada(c_pad, ada_w, ada_b)

    wg16, wu16, wd16 = w_gate.astype(BF16), w_up.astype(BF16), w_down.astype(BF16)
    x2d = x.reshape(T, D)
    for l in range(L):
        mod = mod_all[l, :B]
        sh1, sc1, gt1, sh2, sc2, gt2 = [m.reshape(B, 1, D) for m in jnp.split(mod, 6, axis=-1)]

        h1 = _lnmod(x2d, sc1, sh1, S)
        w_qkv = w_in[l, :, :qkv_cols].astype(BF16)
        f_lo = qkv_cols
        g_lo = qkv_cols + n_fx
        w_rest = jnp.concatenate(
            [w_in[l, :, g_lo:g_lo + 2 * c_cv], w_in[l, :, f_lo:f_lo + n_fx],
             jnp.zeros((D, LANES - n_fx), F32)], axis=1).astype(BF16)
        qkv = _matmul(h1, w_qkv, BF16, "proj_qkv")
        rest = _matmul(h1, w_rest, F32, "proj_glu_forget")

        b_pad = jnp.zeros((1, LANES), F32).at[0, :n_fx].set(b_forget[l])
        cum = _forget_cumsum(rest, b_pad, B, S, (2 * c_cv) // LANES)[:, :n_fx]

        nb = LANES
        o_sb = _sb_attention(qkv, B, S, d_sb // nb, 0, d_sb // nb, 2 * d_sb // nb)
        fx0 = 3 * d_sb // nb
        o_fx = _fox_attention(qkv, cum, B, S, tq, d_fx // nb, fx0, fx0 + d_fx // nb,
                              fx0 + 2 * d_fx // nb)
        o_cv = _conv_module(rest, conv_w[l], conv_b[l], conv_ln_g[l], conv_ln_b[l], B, S)

        rw = jnp.concatenate(
            [r1_w[l], r2_w[l].transpose(1, 0, 2).reshape(D, n_experts),
             jnp.zeros((D, LANES - n_groups - n_experts), F32)], axis=1)
        rw_hi = rw.astype(BF16)
        rw_lo = (rw - rw_hi.astype(F32)).astype(BF16)
        rb = jnp.zeros((1, LANES), F32).at[0, :n_groups].set(r1_b[l])
        rb = rb.at[0, n_groups:n_groups + n_experts].set(r2_b[l].reshape(-1))
        x1, h2, logits = _outproj(
            o_sb, o_fx, o_cv, w_out[l].astype(BF16), x2d, gt1,
            ln1_g[l].reshape(1, D), ln1_b[l].reshape(1, D), sc2, sh2, rw_hi, rw_lo, rb, S, alpha)

        e_lanes, w_lanes = _route(logits, n_groups, epg)
        dest_lanes, be_lanes = _plan(e_lanes, n_experts, n_blocks)
        dest_flat = dest_lanes[:, :2].reshape(-1)
        be = be_lanes[:n_blocks, 0]
        n_used = jnp.sum((be < n_experts).astype(jnp.int32)).reshape(1)
        block_expert = jnp.minimum(be, n_experts - 1)
        slot_tok = _slot_map(dest_flat, n_slots)
        yb = _experts(block_expert, n_used, slot_tok, h2, wg16, wu16, wd16, l, n_blocks)
        x2d = _combine(dest_flat, yb, w_lanes, x1, gt2,
                       ln2_g[l].reshape(1, D), ln2_b[l].reshape(1, D), S, alpha)
    return x2d.reshape(B, S, D)
```

```python
import functools

import jax
import jax.numpy as jnp
from jax import lax
from jax.experimental import pallas as pl
from jax.experimental.pallas import tpu as pltpu

LN_EPS = 1e-5
HEAD_DIM = 64
LANES = 128
SUBLANES = 8
MOE_BLOCK = 256
NEG_BIG = -1e30
SB_DEAD_LOG = -104.0
FOX_PAIRS = 4
VMEM_LIMIT = 56 * 1024 * 1024

F32 = jnp.float32
BF16 = jnp.bfloat16


def _pick(n, cands):
    for c in cands:
        if n % c == 0:
            return c
    return n


def _params(sem):
    return pltpu.CompilerParams(dimension_semantics=sem, vmem_limit_bytes=VMEM_LIMIT)


def _ln_rows(v):
    mu = jnp.mean(v, axis=-1, keepdims=True)
    d = v - mu
    var = jnp.mean(d * d, axis=-1, keepdims=True)
    return d * lax.rsqrt(var + LN_EPS)


def _log_sigmoid(z):
    return jnp.minimum(z, 0.0) - jnp.log1p(jnp.exp(-jnp.abs(z)))


def _sigmoid(z):
    return 1.0 / (1.0 + jnp.exp(-z))


def _split2(v):
    hi = v.astype(BF16)
    lo = (v - hi.astype(F32)).astype(BF16)
    return hi, lo


def _dot(a, b):
    return jnp.dot(a, b, preferred_element_type=F32)


def _dot_nt(a, b):
    return lax.dot_general(a, b, (((1,), (1,)), ((), ())), preferred_element_type=F32)


def _ada_kernel(c_ref, w_ref, b_ref, o_ref):
    c = c_ref[...]
    s = c * _sigmoid(c)
    s_hi, s_lo = _split2(s)
    w_hi, w_lo = _split2(w_ref[0])
    acc = _dot(s_hi, w_hi) + _dot(s_lo, w_hi) + _dot(s_hi, w_lo)
    o_ref[0] = acc + b_ref[0]


def _ada(c_pad, ada_w, ada_b):
    L, D, N = ada_w.shape
    rows = c_pad.shape[0]
    tn = _pick(N, (512, 256, 128))
    return pl.pallas_call(
        _ada_kernel,
        grid=(L, N // tn),
        in_specs=[
            pl.BlockSpec((rows, D), lambda l, n: (0, 0)),
            pl.BlockSpec((1, D, tn), lambda l, n: (l, 0, n)),
            pl.BlockSpec((1, 1, tn), lambda l, n: (l, 0, n)),
        ],
        out_specs=pl.BlockSpec((1, rows, tn), lambda l, n: (l, 0, n)),
        out_shape=jax.ShapeDtypeStruct((L, rows, N), F32),
        compiler_params=_params(("arbitrary", "arbitrary")),
        name="ada_mod",
    )(c_pad, ada_w, ada_b.reshape(L, 1, N))


def _lnmod_kernel(x_ref, sc_ref, sh_ref, o_ref):
    h = _ln_rows(x_ref[...]) * (1.0 + sc_ref[0]) + sh_ref[0]
    o_ref[...] = h.astype(o_ref.dtype)


def _lnmod(x2d, sc, sh, seq):
    T, D = x2d.shape
    tm = _pick(seq, (512, 256, 128))
    per_b = seq // tm
    return pl.pallas_call(
        _lnmod_kernel,
        grid=(T // tm,),
        in_specs=[
            pl.BlockSpec((tm, D), lambda i: (i, 0)),
            pl.BlockSpec((1, 1, D), lambda i: (i // per_b, 0, 0)),
            pl.BlockSpec((1, 1, D), lambda i: (i // per_b, 0, 0)),
        ],
        out_specs=pl.BlockSpec((tm, D), lambda i: (i, 0)),
        out_shape=jax.ShapeDtypeStruct((T, D), BF16),
        compiler_params=_params(("arbitrary",)),
        name="ln_mod",
    )(x2d, sc, sh)


def _mm_kernel(a_ref, b_ref, o_ref):
    o_ref[...] = _dot(a_ref[...], b_ref[...]).astype(o_ref.dtype)


def _matmul(a, w, out_dtype, name):
    M, K = a.shape
    N = w.shape[1]
    tm = _pick(M, (1024, 512, 256, 128))
    tn = _pick(N, (512, 1152, 384, 256, 128))
    return pl.pallas_call(
        _mm_kernel,
        grid=(M // tm, N // tn),
        in_specs=[
            pl.BlockSpec((tm, K), lambda i, j: (i, 0)),
            pl.BlockSpec((K, tn), lambda i, j: (0, j)),
        ],
        out_specs=pl.BlockSpec((tm, tn), lambda i, j: (i, j)),
        out_shape=jax.ShapeDtypeStruct((M, N), out_dtype),
        compiler_params=_params(("arbitrary", "arbitrary")),
        name=name,
    )(a, w)


def _cum_kernel(f_ref, b_ref, o_ref, carry_ref):
    @pl.when(pl.program_id(1) == 0)
    def _():
        carry_ref[...] = jnp.zeros_like(carry_ref)

    ts = f_ref.shape[0]
    lf = _log_sigmoid(f_ref[...] + b_ref[...])
    p1 = lf.astype(BF16)
    r1 = lf - p1.astype(F32)
    p2 = r1.astype(BF16)
    p3 = (r1 - p2.astype(F32)).astype(BF16)
    row = lax.broadcasted_iota(jnp.int32, (ts, ts), 0)
    col = lax.broadcasted_iota(jnp.int32, (ts, ts), 1)
    tri = jnp.where(col <= row, 1.0, 0.0).astype(BF16)
    cum = _dot(tri, p1) + _dot(tri, p2) + _dot(tri, p3) + carry_ref[...]
    o_ref[...] = cum
    carry_ref[...] = cum[ts - 1:ts, :]


def _forget_cumsum(rest, b_pad, batch, seq, col_block):
    T = rest.shape[0]
    ts = _pick(seq, (256, 128))
    ns = seq // ts
    return pl.pallas_call(
        _cum_kernel,
        grid=(batch, ns),
        in_specs=[
            pl.BlockSpec((ts, LANES), lambda b, s: (b * ns + s, col_block)),
            pl.BlockSpec((1, LANES), lambda b, s: (0, 0)),
        ],
        out_specs=pl.BlockSpec((ts, LANES), lambda b, s: (b * ns + s, 0)),
        out_shape=jax.ShapeDtypeStruct((T, LANES), F32),
        scratch_shapes=[pltpu.VMEM((1, LANES), F32)],
        compiler_params=_params(("arbitrary", "arbitrary")),
        name="forget_cumsum",
    )(rest, b_pad)


def _masked_heads(q_ref, pairs):
    lane = lax.broadcasted_iota(jnp.int32, (1, LANES), 1)
    heads = []
    for p in range(pairs):
        q2 = q_ref[:, p * LANES:(p + 1) * LANES] * jnp.asarray(HEAD_DIM ** -0.5, q_ref.dtype)
        zero = jnp.zeros_like(q2)
        heads += [jnp.where(lane < HEAD_DIM, q2, zero), jnp.where(lane >= HEAD_DIM, q2, zero)]
    return heads


def _pairs_per_step(cands, *counts):
    return next(c for c in cands if all(n % c == 0 for n in counts))


def _store_value_transposed(v_ref, vt_ref, first):
    pairs, nk, _, two_tk = vt_ref.shape
    tk = two_tk // 2
    for p in range(pairs):
        for j in range(nk):
            vt = v_ref[j * tk:(j + 1) * tk, p * LANES:(p + 1) * LANES].astype(F32).T
            both = jnp.concatenate([jnp.where(first, vt, 0.0), jnp.where(first, 0.0, vt)], axis=1)
            vt_ref[p, j] = both.astype(BF16)


def _sb_kernel(q_ref, k_ref, v_ref, o_ref, vt_ref, *, pairs):
    tq = q_ref.shape[0]
    i = pl.program_id(2)
    q_heads = _masked_heads(q_ref, pairs)
    first = lax.broadcasted_iota(jnp.int32, (LANES, 1), 0) < HEAD_DIM

    @pl.when(i == 0)
    def _():
        for p in range(pairs):
            for j in range(vt_ref.shape[1]):
                vt = v_ref[j * tq:(j + 1) * tq, p * LANES:(p + 1) * LANES].astype(F32).T
                vt_ref[p, j] = vt.astype(BF16)

    q_pair = [jnp.concatenate([q_heads[2 * p], q_heads[2 * p + 1]], axis=0) for p in range(pairs)]
    key = lax.broadcasted_iota(jnp.int32, (tq, tq), 0)
    qry = lax.broadcasted_iota(jnp.int32, (tq, tq), 1)
    later_keys = jnp.where(qry > key, 1.0, 0.0).astype(BF16)
    causal = jnp.concatenate([key < qry, key < qry], axis=1)

    def block(j, state, diagonal):
        start = pl.multiple_of(j * tq, tq)
        out = []
        scores = [_dot_nt(k_ref[pl.ds(start, tq), p * LANES:(p + 1) * LANES], q_pair[p])
                  for p in range(pairs)]
        log_sig, log_keep = [], []
        for z in scores:
            ls = jnp.minimum(z, 0.0) - jnp.log(1.0 + jnp.exp(-jnp.abs(z)))
            lk = ls - z
            if diagonal:
                lk = jnp.where(causal, lk, 0.0)
            log_sig.append(ls)
            log_keep.append(lk)
        later = []
        for lk in log_keep:
            lk_hi, lk_lo = _split2(lk)
            later.append(_dot(later_keys, lk_hi) + _dot(later_keys, lk_lo))
        weights = []
        for p in range(pairs):
            w = jnp.exp(log_sig[p] + later[p] + state[2 * p])
            if diagonal:
                w = jnp.where(causal, w, 0.0)
            weights.append(w.astype(BF16))
        for p in range(pairs):
            pv = _dot(vt_ref[p, j], weights[p])
            acc = state[2 * p + 1] + jnp.where(first, pv[:, :tq], pv[:, tq:])
            out += [state[2 * p] + jnp.sum(log_keep[p], axis=0, keepdims=True), acc]
        return tuple(out)

    def live(state):
        top = state[0]
        for p in range(1, pairs):
            top = jnp.maximum(top, state[2 * p])
        return (jnp.max(top) > SB_DEAD_LOG).astype(jnp.int32)

    init = (jnp.zeros((1, 2 * tq), F32), jnp.zeros((LANES, tq), F32)) * pairs
    state = block(i, init, True)

    def cond(carry):
        return (carry[0] >= 0) & (carry[1] > 0)

    def body(carry):
        state = block(carry[0], carry[2:], False)
        return (carry[0] - 1, live(state)) + state

    res = lax.while_loop(cond, body, (i - 1, live(state)) + state)
    slabs = [res[2 + 2 * p + 1].T for p in range(pairs)]
    o = slabs[0] if pairs == 1 else jnp.concatenate(slabs, axis=1)
    o_ref[...] = o.astype(o_ref.dtype)


def _sb_attention(qkv, batch, seq, n_pairs, q_col, k_col, v_col):
    T = qkv.shape[0]
    tq = _pick(seq, (256, 128))
    nq = seq // tq
    pairs = _pairs_per_step((4, 2, 1), n_pairs, q_col, k_col, v_col)
    W = pairs * LANES
    qc, kc, vc = q_col // pairs, k_col // pairs, v_col // pairs
    return pl.pallas_call(
        functools.partial(_sb_kernel, pairs=pairs),
        grid=(batch, n_pairs // pairs, nq),
        in_specs=[
            pl.BlockSpec((tq, W), lambda b, p, i: (b * nq + i, qc + p)),
            pl.BlockSpec((seq, W), lambda b, p, i: (b, kc + p)),
            pl.BlockSpec((seq, W), lambda b, p, i: (b, vc + p)),
        ],
        out_specs=pl.BlockSpec((tq, W), lambda b, p, i: (b * nq + i, p)),
        out_shape=jax.ShapeDtypeStruct((T, n_pairs * LANES), BF16),
        scratch_shapes=[pltpu.VMEM((pairs, nq, LANES, tq), BF16)],
        compiler_params=_params(("arbitrary", "arbitrary", "arbitrary")),
        name="sb_attention",
    )(qkv, qkv, qkv)


def _fox_kernel(q_ref, k_ref, v_ref, cc_ref, o_ref, vt_ref, *, pairs):
    tq = q_ref.shape[0]
    i = pl.program_id(2)
    q_heads = _masked_heads(q_ref, pairs)
    first = lax.broadcasted_iota(jnp.int32, (LANES, 1), 0) < HEAD_DIM

    @pl.when(i == 0)
    def _():
        _store_value_transposed(v_ref, vt_ref, first)

    key = lax.broadcasted_iota(jnp.int32, (tq, tq), 0)
    qry = lax.broadcasted_iota(jnp.int32, (tq, tq), 1)

    def block(j, carry, diagonal):
        start = pl.multiple_of(j * tq, tq)
        out = []
        scores = [_dot_nt(k_ref[pl.ds(start, tq), (h // 2) * LANES:(h // 2 + 1) * LANES], q_heads[h])
                  for h in range(2 * pairs)]
        shifted = []
        for h in range(2 * pairs):
            p, s = divmod(h, 2)
            logits = scores[h] - cc_ref[0, pl.ds(start, tq), h:h + 1]
            if diagonal:
                logits = jnp.where(key <= qry, logits, NEG_BIG)
            m_new = jnp.maximum(carry[5 * p + 2 * s], jnp.max(logits, axis=0, keepdims=True))
            shifted.append((logits, m_new))
        stats = []
        for h in range(2 * pairs):
            p, s = divmod(h, 2)
            logits, m_new = shifted[h]
            alpha = jnp.exp(carry[5 * p + 2 * s] - m_new)
            prob = jnp.exp(logits - m_new)
            l = carry[5 * p + 2 * s + 1] * alpha + jnp.sum(prob, axis=0, keepdims=True)
            stats.append((m_new, l, alpha, prob.astype(BF16)))
        for p in range(pairs):
            sa, sb = stats[2 * p], stats[2 * p + 1]
            pv = _dot(vt_ref[p, j], jnp.concatenate([sa[3], sb[3]], axis=0))
            acc = carry[5 * p + 4] * jnp.where(first, sa[2], sb[2]) + pv
            out += [sa[0], sa[1], sb[0], sb[1], acc]
        return tuple(out)

    row_stat = (jnp.full((1, tq), NEG_BIG, F32), jnp.zeros((1, tq), F32))
    init = (row_stat * 2 + (jnp.zeros((LANES, tq), F32),)) * pairs
    state = block(i, init, True)
    res = lax.fori_loop(0, i, lambda j, c: block(j, c, False), state)
    slabs = []
    for p in range(pairs):
        inv = jnp.where(first, 1.0 / res[5 * p + 1], 1.0 / res[5 * p + 3])
        slabs.append((res[5 * p + 4] * inv).T)
    o = slabs[0] if pairs == 1 else jnp.concatenate(slabs, axis=1)
    o_ref[...] = o.astype(o_ref.dtype)


def _fox_attention(qkv, cum, batch, seq, tq, n_pairs, q_col, k_col, v_col):
    T = qkv.shape[0]
    nq = seq // tq
    pairs = _pairs_per_step((FOX_PAIRS, 2, 1), n_pairs, q_col, k_col, v_col)
    W = pairs * LANES
    qc, kc, vc = q_col // pairs, k_col // pairs, v_col // pairs
    return pl.pallas_call(
        functools.partial(_fox_kernel, pairs=pairs),
        grid=(batch, n_pairs // pairs, nq),
        in_specs=[
            pl.BlockSpec((tq, W), lambda b, p, i: (b * nq + i, qc + p)),
            pl.BlockSpec((seq, W), lambda b, p, i: (b, kc + p)),
            pl.BlockSpec((seq, W), lambda b, p, i: (b, vc + p)),
            pl.BlockSpec((1, seq, 2 * pairs), lambda b, p, i: (p, b, 0)),
        ],
        out_specs=pl.BlockSpec((tq, W), lambda b, p, i: (b * nq + i, p)),
        out_shape=jax.ShapeDtypeStruct((T, n_pairs * LANES), BF16),
        scratch_shapes=[pltpu.VMEM((pairs, nq, LANES, 2 * tq), BF16)],
        compiler_params=_params(("arbitrary", "arbitrary", "arbitrary")),
        name="fox_attention",
    )(qkv, qkv, qkv, cum.reshape(T, n_pairs // pairs, 2 * pairs).transpose(1, 0, 2))


def _conv_kernel(a_ref, g_ref, w_ref, cb_ref, lg_ref, lb_ref, o_ref, u_ref, s_ref, *, width, halo):
    ts = a_ref.shape[0]

    @pl.when(pl.program_id(1) == 0)
    def _():
        u_ref[0:halo, :] = jnp.zeros((halo, u_ref.shape[1]), F32)

    u_ref[halo:halo + ts, :] = a_ref[...] * _sigmoid(g_ref[...])
    span = ts + halo - SUBLANES
    for r in range(1, SUBLANES):
        s_ref[r - 1, 0:span, :] = u_ref[r:r + span, :]
    acc = jnp.zeros(a_ref.shape, F32) + cb_ref[...]
    for k in range(width):
        base, r = divmod(halo - (width - 1) + k, SUBLANES)
        lo = base * SUBLANES
        tap = u_ref[lo:lo + ts, :] if r == 0 else s_ref[r - 1, lo:lo + ts, :]
        acc = acc + w_ref[k:k + 1, :] * tap
    y = _ln_rows(acc) * lg_ref[...] + lb_ref[...]
    o_ref[...] = (y * _sigmoid(y)).astype(o_ref.dtype)
    u_ref[0:halo, :] = u_ref[ts:ts + halo, :]


def _conv_module(rest, conv_w, conv_b, ln_g, ln_b, batch, seq):
    T = rest.shape[0]
    width, C = conv_w.shape
    halo = -(-(width - 1) // SUBLANES) * SUBLANES
    ts = _pick(seq, (256, 128))
    ns = seq // ts
    kern = functools.partial(_conv_kernel, width=width, halo=halo)
    vec = lambda: pl.BlockSpec((1, C), lambda b, s: (0, 0))
    return pl.pallas_call(
        kern,
        grid=(batch, ns),
        in_specs=[
            pl.BlockSpec((ts, C), lambda b, s: (b * ns + s, 0)),
            pl.BlockSpec((ts, C), lambda b, s: (b * ns + s, 1)),
            pl.BlockSpec((width, C), lambda b, s: (0, 0)),
            vec(), vec(), vec(),
        ],
        out_specs=pl.BlockSpec((ts, C), lambda b, s: (b * ns + s, 0)),
        out_shape=jax.ShapeDtypeStruct((T, C), BF16),
        scratch_shapes=[pltpu.VMEM((ts + halo, C), F32),
                        pltpu.VMEM((SUBLANES - 1, ts + halo, C), F32)],
        compiler_params=_params(("arbitrary", "arbitrary")),
        name="conformer_conv",
    )(rest, rest, conv_w, conv_b.reshape(1, C), ln_g.reshape(1, C), ln_b.reshape(1, C))


def _outproj_kernel(osb_ref, ofx_ref, ocv_ref, w_ref, x_ref, gt_ref, g_ref, b_ref,
                    sc_ref, sh_ref, rwh_ref, rwl_ref, rb_ref,
                    x1_ref, h2_ref, lg_ref, *, alpha):
    d_sb = osb_ref.shape[1]
    d_fx = ofx_ref.shape[1]
    half = x_ref.shape[0] // 2
    ys = []
    for r in range(2):
        rows = slice(r * half, (r + 1) * half)
        y = _dot(osb_ref[rows, :], w_ref[0:d_sb, :])
        y = y + _dot(ofx_ref[rows, :], w_ref[d_sb:d_sb + d_fx, :])
        ys.append(y + _dot(ocv_ref[rows, :], w_ref[d_sb + d_fx:, :]))
    for r in range(2):
        rows = slice(r * half, (r + 1) * half)
        x1 = _ln_rows(alpha * x_ref[rows, :] + gt_ref[0] * ys[r]) * g_ref[...] + b_ref[...]
        x1_ref[rows, :] = x1
        h2 = _ln_rows(x1) * (1.0 + sc_ref[0]) + sh_ref[0]
        h2_ref[rows, :] = h2
        h_hi, h_lo = _split2(h2)
        lg = _dot(h_hi, rwh_ref[...]) + _dot(h_lo, rwh_ref[...]) + _dot(h_hi, rwl_ref[...])
        lg_ref[rows, :] = lg + rb_ref[...]


def _outproj(o_sb, o_fx, o_cv, w_out, x2d, gt, g, b, sc2, sh2, rw_hi, rw_lo, rb, seq, alpha):
    T, D = x2d.shape
    tm = _pick(seq, (512, 256, 128))
    per_b = seq // tm
    rowblk = lambda n: pl.BlockSpec((tm, n), lambda i: (i, 0))
    full = lambda r, c: pl.BlockSpec((r, c), lambda i: (0, 0))
    perb = lambda: pl.BlockSpec((1, 1, D), lambda i: (i // per_b, 0, 0))
    return pl.pallas_call(
        functools.partial(_outproj_kernel, alpha=alpha),
        grid=(T // tm,),
        in_specs=[
            rowblk(o_sb.shape[1]), rowblk(o_fx.shape[1]), rowblk(o_cv.shape[1]),
            full(D, D), rowblk(D), perb(), full(1, D), full(1, D), perb(), perb(),
            full(D, LANES), full(D, LANES), full(1, LANES),
        ],
        out_specs=[rowblk(D), rowblk(D), rowblk(LANES)],
        out_shape=[jax.ShapeDtypeStruct((T, D), F32), jax.ShapeDtypeStruct((T, D), F32),
                   jax.ShapeDtypeStruct((T, LANES), F32)],
        compiler_params=_params(("arbitrary",)),
        name="out_proj_norm_router",
    )(o_sb, o_fx, o_cv, w_out, x2d, gt, g, b, sc2, sh2, rw_hi, rw_lo, rb)


def _route_kernel(lg_ref, e_ref, w_ref, *, n_groups, epg):
    L = lg_ref[...]
    lane_i = lax.broadcasted_iota(jnp.int32, L.shape, 1)
    lane = lane_i.astype(F32)
    far = 1e6
    is_grp = lane < n_groups
    l1 = jnp.where(is_grp, L, NEG_BIG)
    m1 = jnp.max(l1, axis=1, keepdims=True)
    grp = jnp.min(jnp.where(is_grp & (l1 == m1), lane, far), axis=1, keepdims=True)
    s1 = jnp.sum(jnp.where(is_grp, jnp.exp(l1 - m1), 0.0), axis=1, keepdims=True)
    p_grp = 1.0 / s1
    lo = n_groups + grp * epg
    in_grp = (lane >= lo) & (lane < lo + epg)
    l2 = jnp.where(in_grp, L, NEG_BIG)
    m2 = jnp.max(l2, axis=1, keepdims=True)
    e2 = jnp.where(in_grp, jnp.exp(l2 - m2), -1.0)
    v1 = jnp.max(e2, axis=1, keepdims=True)
    i1 = jnp.min(jnp.where(e2 == v1, lane, far), axis=1, keepdims=True)
    e2b = jnp.where(lane == i1, -1.0, e2)
    v2 = jnp.max(e2b, axis=1, keepdims=True)
    i2 = jnp.min(jnp.where(e2b == v2, lane, far), axis=1, keepdims=True)
    den = v1 + v2
    w0 = p_grp * v1 / den
    w1 = p_grp * v2 / den
    ids = jnp.where(lane_i == 0, i1 - n_groups, jnp.where(lane_i == 1, i2 - n_groups, 0.0))
    e_ref[...] = ids.astype(jnp.int32)
    w_ref[...] = jnp.where(lane_i == 0, w0, jnp.where(lane_i == 1, w1, 0.0))


def _route(logits, n_groups, epg):
    T = logits.shape[0]
    tm = _pick(T, (512, 256, 128))
    blk = lambda: pl.BlockSpec((tm, LANES), lambda i: (i, 0))
    return pl.pallas_call(
        functools.partial(_route_kernel, n_groups=n_groups, epg=epg),
        grid=(T // tm,),
        in_specs=[blk()],
        out_specs=[blk(), blk()],
        out_shape=[jax.ShapeDtypeStruct((T, LANES), jnp.int32),
                   jax.ShapeDtypeStruct((T, LANES), F32)],
        compiler_params=_params(("arbitrary",)),
        name="route_topk",
    )(logits)


def _plan_kernel(e_ref, dest_ref, be_ref, cnt_ref, base_ref, *, n_experts):
    p = pl.program_id(0)
    i = pl.program_id(1)
    tm = e_ref.shape[0]
    lane = lax.broadcasted_iota(jnp.int32, (tm, LANES), 1)
    e = e_ref[...]
    hot0 = lane == e[:, 0:1]
    hot1 = lane == e[:, 1:2]
    both = jnp.where(hot0 | hot1, 1.0, 0.0)

    @pl.when((p == 0) & (i == 0))
    def _():
        cnt_ref[...] = jnp.zeros_like(cnt_ref)

    @pl.when(p == 0)
    def _():
        cnt_ref[...] += jnp.sum(both, axis=0, keepdims=True)

    @pl.when((p == 1) & (i == 0))
    def _():
        nblk = jnp.floor((cnt_ref[...] + (MOE_BLOCK - 1)) * (1.0 / MOE_BLOCK))
        r = lax.broadcasted_iota(jnp.int32, (LANES, LANES), 0)
        c = lax.broadcasted_iota(jnp.int32, (LANES, LANES), 1)
        before = jnp.where(r < c, 1.0, 0.0).astype(BF16)
        nb8 = jnp.broadcast_to(nblk, (SUBLANES, LANES)).astype(BF16)
        excl = _dot(nb8, before)[0:1, :]
        base_ref[...] = excl * MOE_BLOCK
        cnt_ref[...] = jnp.zeros_like(cnt_ref)
        incl = excl + nblk
        nb_rows = be_ref.shape[0]
        bidx = lax.broadcasted_iota(jnp.int32, (nb_rows, LANES), 0).astype(F32)
        lane_b = lax.broadcasted_iota(jnp.int32, (nb_rows, LANES), 1)
        done = jnp.where((incl <= bidx) & (lane_b < n_experts), 1.0, 0.0)
        be = jnp.sum(done, axis=1, keepdims=True)
        be_ref[...] = jnp.broadcast_to(be, (nb_rows, LANES)).astype(jnp.int32)

    @pl.when(p == 1)
    def _():
        r = lax.broadcasted_iota(jnp.int32, (tm, tm), 0)
        c = lax.broadcasted_iota(jnp.int32, (tm, tm), 1)
        earlier = jnp.where(c < r, 1.0, 0.0).astype(BF16)
        tot = _dot(earlier, both.astype(BF16)) + cnt_ref[...] + base_ref[...]
        d0 = jnp.sum(jnp.where(hot0, tot, 0.0), axis=1, keepdims=True)
        d1 = jnp.sum(jnp.where(hot1, tot, 0.0), axis=1, keepdims=True)
        dest = jnp.where(lane == 0, d0, jnp.where(lane == 1, d1, 0.0))
        dest_ref[...] = dest.astype(jnp.int32)
        cnt_ref[...] += jnp.sum(both, axis=0, keepdims=True)


def _plan(e_lanes, n_experts, n_blocks):
    T = e_lanes.shape[0]
    tm = _pick(T, (256, 128))
    nb_rows = -(-n_blocks // SUBLANES) * SUBLANES
    return pl.pallas_call(
        functools.partial(_plan_kernel, n_experts=n_experts),
        grid=(2, T // tm),
        in_specs=[pl.BlockSpec((tm, LANES), lambda p, i: (i, 0))],
        out_specs=[pl.BlockSpec((tm, LANES), lambda p, i: (i * p, 0)),
                   pl.BlockSpec((nb_rows, LANES), lambda p, i: (0, 0))],
        out_shape=[jax.ShapeDtypeStruct((T, LANES), jnp.int32),
                   jax.ShapeDtypeStruct((nb_rows, LANES), jnp.int32)],
        scratch_shapes=[pltpu.VMEM((1, LANES), F32), pltpu.VMEM((1, LANES), F32)],
        compiler_params=_params(("arbitrary", "arbitrary")),
        name="dispatch_plan",
    )(e_lanes)


def _row_copy(src, src_row, dst, dst_row, sem):
    return pltpu.make_async_copy(src.at[pl.ds(src_row, 1)], dst.at[pl.ds(dst_row, 1)], sem)


def _slotmap_kernel(dest_ref, tok_ref):
    def clear(s, carry):
        tok_ref[s] = 0
        return carry

    lax.fori_loop(0, tok_ref.shape[0], clear, 0, unroll=8)

    def put(t, carry):
        tok_ref[dest_ref[2 * t]] = t
        tok_ref[dest_ref[2 * t + 1]] = t
        return carry

    lax.fori_loop(0, dest_ref.shape[0] // 2, put, 0, unroll=8)


def _slot_map(dest_flat, n_slots):
    return pl.pallas_call(
        _slotmap_kernel,
        in_specs=[pl.BlockSpec(memory_space=pltpu.SMEM)],
        out_specs=pl.BlockSpec(memory_space=pltpu.SMEM),
        out_shape=jax.ShapeDtypeStruct((n_slots,), jnp.int32),
        name="moe_slot_map",
    )(dest_flat)


def _expert_kernel(be_ref, nu_ref, tok_ref, h_ref, wg_ref, wu_ref, wd_ref, o_ref, xbuf, sems):
    del be_ref
    b = pl.program_id(0)
    n_used = nu_ref[0]
    slot = b % 2

    def issue(blk, s, r):
        _row_copy(h_ref, tok_ref[blk * MOE_BLOCK + r], xbuf.at[s], r, sems.at[s]).start()

    @pl.when(b == 0)
    def _():
        def step(r, carry):
            issue(0, 0, r)
            return carry

        lax.fori_loop(0, MOE_BLOCK, step, 0, unroll=8)

    @pl.when(b + 1 < n_used)
    def _():
        for r in range(MOE_BLOCK):
            issue(b + 1, 1 - slot, r)

    @pl.when(b < n_used)
    def _():
        def drain(r, carry):
            _row_copy(h_ref, 0, xbuf.at[slot], 0, sems.at[slot]).wait()
            return carry

        lax.fori_loop(0, MOE_BLOCK, drain, 0, unroll=8)
        x = xbuf[slot].astype(BF16)
        g = _dot(x, wg_ref[0, 0])
        u = _dot(x, wu_ref[0, 0])
        hid = (g * _sigmoid(g)) * u
        o_ref[...] = _dot(hid.astype(BF16), wd_ref[0, 0])

    @pl.when(b >= n_used)
    def _():
        o_ref[...] = jnp.zeros_like(o_ref)


def _experts(block_expert, n_used, slot_tok, h2, w_gate, w_up, w_down, layer, n_blocks):
    D = h2.shape[1]
    DE = w_gate.shape[3]
    wmap = lambda b, be, nu, tok: (layer, be[jnp.minimum(b, nu[0] - 1)], 0, 0)
    return pl.pallas_call(
        _expert_kernel,
        grid_spec=pltpu.PrefetchScalarGridSpec(
            num_scalar_prefetch=3,
            grid=(n_blocks,),
            in_specs=[
                pl.BlockSpec(memory_space=pl.ANY),
                pl.BlockSpec((1, 1, D, DE), wmap),
                pl.BlockSpec((1, 1, D, DE), wmap),
                pl.BlockSpec((1, 1, DE, D), wmap),
            ],
            out_specs=pl.BlockSpec((MOE_BLOCK, D), lambda b, be, nu, tok: (b, 0)),
            scratch_shapes=[pltpu.VMEM((2, MOE_BLOCK, D), F32), pltpu.SemaphoreType.DMA((2,))],
        ),
        out_shape=jax.ShapeDtypeStruct((n_blocks * MOE_BLOCK, D), F32),
        compiler_params=_params(("arbitrary",)),
        name="moe_experts",
    )(block_expert, n_used, slot_tok, h2, w_gate, w_up, w_down)


def _combine_kernel(dest_ref, yb_ref, w_ref, x_ref, gt_ref, g_ref, b_ref, o_ref,
                    buf_ref, sems, *, alpha):
    tm = x_ref.shape[0]
    i = pl.program_id(0)
    slot = i % 2

    def issue(tile, s, r):
        t = tile * tm + r
        _row_copy(yb_ref, dest_ref[2 * t], buf_ref.at[s, 0], r, sems.at[s]).start()
        _row_copy(yb_ref, dest_ref[2 * t + 1], buf_ref.at[s, 1], r, sems.at[s]).start()

    @pl.when(i == 0)
    def _():
        def step(r, carry):
            issue(0, 0, r)
            return carry

        lax.fori_loop(0, tm, step, 0, unroll=8)

    @pl.when(i + 1 < pl.num_programs(0))
    def _():
        for r in range(tm):
            issue(i + 1, 1 - slot, r)

    def drain(r, carry):
        _row_copy(yb_ref, 0, buf_ref.at[slot, 0], 0, sems.at[slot]).wait()
        return carry

    lax.fori_loop(0, 2 * tm, drain, 0, unroll=8)
    w = w_ref[...]
    y = w[:, 0:1] * buf_ref[slot, 0] + w[:, 1:2] * buf_ref[slot, 1]
    v = alpha * x_ref[...] + gt_ref[0] * y
    o_ref[...] = _ln_rows(v) * g_ref[...] + b_ref[...]


def _combine(dest_flat, yb, w_lanes, x1, gt, g, b, seq, alpha):
    T, D = x1.shape
    tm = _pick(seq, (256, 128))
    per_b = seq // tm
    return pl.pallas_call(
        functools.partial(_combine_kernel, alpha=alpha),
        grid_spec=pltpu.PrefetchScalarGridSpec(
            num_scalar_prefetch=1,
            grid=(T // tm,),
            in_specs=[
                pl.BlockSpec(memory_space=pl.ANY),
                pl.BlockSpec((tm, LANES), lambda i, d: (i, 0)),
                pl.BlockSpec((tm, D), lambda i, d: (i, 0)),
                pl.BlockSpec((1, 1, D), lambda i, d: (i // per_b, 0, 0)),
                pl.BlockSpec((1, D), lambda i, d: (0, 0)),
                pl.BlockSpec((1, D), lambda i, d: (0, 0)),
            ],
            out_specs=pl.BlockSpec((tm, D), lambda i, d: (i, 0)),
            scratch_shapes=[pltpu.VMEM((2, 2, tm, D), F32), pltpu.SemaphoreType.DMA((2,))],
        ),
        out_shape=jax.ShapeDtypeStruct((T, D), F32),
        compiler_params=_params(("arbitrary",)),
        name="moe_combine_norm",
    )(dest_flat, yb, w_lanes, x1, gt, g, b)


def kernel(x, c, ada_w, ada_b, w_in, b_forget, conv_w, conv_b, conv_ln_g, conv_ln_b, w_out,
           ln1_g, ln1_b, r1_w, r1_b, r2_w, r2_b, w_gate, w_up, w_down, ln2_g, ln2_b):
    B, S, D = x.shape
    L = ada_w.shape[0]
    T = B * S
    alpha = float((2 * L) ** 0.25)
    d_sb, d_fx, c_cv = D // 4, D // 2, D // 4
    n_fx = d_fx // HEAD_DIM
    n_groups = r1_w.shape[-1]
    epg = r2_w.shape[-1]
    n_experts = n_groups * epg
    n_blocks = (2 * T) // MOE_BLOCK + n_experts
    n_slots = n_blocks * MOE_BLOCK
    qkv_cols = 3 * d_sb + 3 * d_fx
    tq = _pick(S, (512, 256, 128))

    rows = -(-B // SUBLANES) * SUBLANES
    c_pad = jnp.zeros((rows, D), F32).at[:B].set(c)
    mod_all = _ada(c_pad, ada_w, ada_b)

    wg16, wu16, wd16 = w_gate.astype(BF16), w_up.astype(BF16), w_down.astype(BF16)
    x2d = x.reshape(T, D)
    for l in range(L):
        mod = mod_all[l, :B]
        sh1, sc1, gt1, sh2, sc2, gt2 = [m.reshape(B, 1, D) for m in jnp.split(mod, 6, axis=-1)]

        h1 = _lnmod(x2d, sc1, sh1, S)
        w_qkv = w_in[l, :, :qkv_cols].astype(BF16)
        f_lo = qkv_cols
        g_lo = qkv_cols + n_fx
        w_rest = jnp.concatenate(
            [w_in[l, :, g_lo:g_lo + 2 * c_cv], w_in[l, :, f_lo:f_lo + n_fx],
             jnp.zeros((D, LANES - n_fx), F32)], axis=1).astype(BF16)
        qkv = _matmul(h1, w_qkv, BF16, "proj_qkv")
        rest = _matmul(h1, w_rest, F32, "proj_glu_forget")

        b_pad = jnp.zeros((1, LANES), F32).at[0, :n_fx].set(b_forget[l])
        cum = _forget_cumsum(rest, b_pad, B, S, (2 * c_cv) // LANES)[:, :n_fx]

        nb = LANES
        o_sb = _sb_attention(qkv, B, S, d_sb // nb, 0, d_sb // nb, 2 * d_sb // nb)
        fx0 = 3 * d_sb // nb
        o_fx = _fox_attention(qkv, cum, B, S, tq, d_fx // nb, fx0, fx0 + d_fx // nb,
                              fx0 + 2 * d_fx // nb)
        o_cv = _conv_module(rest, conv_w[l], conv_b[l], conv_ln_g[l], conv_ln_b[l], B, S)

        rw = jnp.concatenate(
            [r1_w[l], r2_w[l].transpose(1, 0, 2).reshape(D, n_experts),
             jnp.zeros((D, LANES - n_groups - n_experts), F32)], axis=1)
        rw_hi = rw.astype(BF16)
        rw_lo = (rw - rw_hi.astype(F32)).astype(BF16)
        rb = jnp.zeros((1, LANES), F32).at[0, :n_groups].set(r1_b[l])
        rb = rb.at[0, n_groups:n_groups + n_experts].set(r2_b[l].reshape(-1))
        x1, h2, logits = _outproj(
            o_sb, o_fx, o_cv, w_out[l].astype(BF16), x2d, gt1,
            ln1_g[l].reshape(1, D), ln1_b[l].reshape(1, D), sc2, sh2, rw_hi, rw_lo, rb, S, alpha)

        e_lanes, w_lanes = _route(logits, n_groups, epg)
        dest_lanes, be_lanes = _plan(e_lanes, n_experts, n_blocks)
        dest_flat = dest_lanes[:, :2].reshape(-1)
        be = be_lanes[:n_blocks, 0]
        n_used = jnp.sum((be < n_experts).astype(jnp.int32)).reshape(1)
        block_expert = jnp.minimum(be, n_experts - 1)
        slot_tok = _slot_map(dest_flat, n_slots)
        yb = _experts(block_expert, n_used, slot_tok, h2, wg16, wu16, wd16, l, n_blocks)
        x2d = _combine(dest_flat, yb, w_lanes, x1, gt2,
                       ln2_g[l].reshape(1, D), ln2_b[l].reshape(1, D), S, alpha)
    return x2d.reshape(B, S, D)
```
